```python
import jax, jax.numpy as jnp
from jax import lax
import numpy as np

D_MODEL = 1024
BATCH = 8
SEQ = 4096
DEPTH = 2
DEC_BATCH = 32
DEC_SEQ = 8
PAST_LEN = 16384
PAGE_SIZE = 128

HEAD_DIM = 64
FOX_HEADS = 8
NSA_HEADS = 8
NSA_KV = 2
NSA_HPG = NSA_HEADS // NSA_KV
FOX_W = FOX_HEADS * HEAD_DIM
NSA_W = NSA_HEADS * HEAD_DIM
MIX_W = FOX_W + NSA_W
NSA_KV_W = NSA_KV * HEAD_DIM
IN_SIZES = (FOX_W, FOX_W, FOX_W, FOX_HEADS, NSA_W, 6 * NSA_KV_W, 3 * NSA_HEADS)
IN_COLS = 3 * FOX_W + FOX_HEADS + NSA_W + 6 * NSA_KV_W + 3 * NSA_HEADS
CMP_LEN = 32
CMP_STRIDE = 16
SEL_BLOCK = 64
N_SELECT = 16
WINDOW = 512
FORCED_SCORE = 1e4
ROPE_THETA = 10000.0
Q_BLOCK = 128
ATTN_SCALE = HEAD_DIM ** -0.5
FORGET_BIAS = 3.0
PEER_KEYS = 128
PEER_EXPERTS = PEER_KEYS * PEER_KEYS
PEER_HEADS = 8
PEER_QDIM = 256
PEER_TOPK = 16
PEER_TOK_BLOCK = 128
RMS_EPS = 1e-6

kernel_name = 'hymba_fox_nsa_peer_step'


def rms_norm(x, g):
    xf = x.astype(jnp.float32)
    y = xf * lax.rsqrt(jnp.mean(xf * xf, axis=-1, keepdims=True) + RMS_EPS)
    return (y * g.astype(jnp.float32)).astype(x.dtype)


def rope(x, pos):
    half = HEAD_DIM // 2
    inv = ROPE_THETA ** (-jnp.arange(half, dtype=jnp.float32) / half)
    ang = pos.astype(jnp.float32)[:, None] * inv[None, :]
    cos = jnp.cos(ang)[:, None, :]
    sin = jnp.sin(ang)[:, None, :]
    xf = x.astype(jnp.float32)
    x1, x2 = xf[..., :half], xf[..., half:]
    return jnp.concatenate([x1 * cos - x2 * sin, x2 * cos + x1 * sin], axis=-1).astype(x.dtype)


def masked_softmax(s, mask):
    s = jnp.where(mask, s.astype(jnp.float32), -jnp.inf)
    m = jnp.max(s, axis=-1, keepdims=True)
    m = jnp.where(jnp.isfinite(m), m, 0.0)
    e = jnp.where(mask, jnp.exp(s - m), 0.0)
    return e / jnp.maximum(jnp.sum(e, axis=-1, keepdims=True), 1e-30)


def project(h, w_in, fox_fb):
    p = h @ w_in
    offs, acc = [], 0
    for size in IN_SIZES[:-1]:
        acc += size
        offs.append(acc)
    fq, fk, fv, ff, nq, nkv, ng = jnp.split(p, offs, axis=-1)
    lead = h.shape[:-1]
    fq = fq.reshape(lead + (FOX_HEADS, HEAD_DIM))
    fk = fk.reshape(lead + (FOX_HEADS, HEAD_DIM))
    fv = fv.reshape(lead + (FOX_HEADS, HEAD_DIM))
    logf = jax.nn.log_sigmoid(ff.astype(jnp.float32) + fox_fb.astype(jnp.float32))
    nq = nq.reshape(lead + (NSA_HEADS, HEAD_DIM))
    nkv = nkv.reshape(lead + (6, NSA_KV, HEAD_DIM))
    gate = jax.nn.sigmoid(ng.astype(jnp.float32)).reshape(lead + (NSA_KV, NSA_HPG, 3))
    return fq, fk, fv, logf, nq, nkv, gate


def nsa_rows(nq, nkv, pos):
    lead = nq.shape[:-2]
    q = rope(nq, pos).reshape(lead + (NSA_KV, NSA_HPG, HEAD_DIM))
    k = rope(nkv[..., 0::2, :, :].reshape(lead + (3 * NSA_KV, HEAD_DIM)), pos).reshape(lead + (3, NSA_KV, HEAD_DIM))
    rows = jnp.stack([k, nkv[..., 1::2, :, :]], axis=-3).reshape(lead + (6, NSA_KV, HEAD_DIM))
    return q, rows


def compress(x, pe, w1, w2):
    B, L, G, D = x.shape
    R = CMP_LEN // CMP_STRIDE
    nch = L // CMP_STRIDE
    nc = nch - R + 1
    ch = x[:, :nch * CMP_STRIDE].reshape(B, nch, CMP_STRIDE, G, D)
    pe_r = pe.reshape(R, CMP_STRIDE, D)
    w1_r = w1.reshape(R, CMP_STRIDE, D, D)
    hsum = sum(jnp.einsum('bcsgd,sde->bcge', ch[:, r:r + nc] + pe_r[r][None, None, :, None, :], w1_r[r])
               for r in range(R))
    return jax.nn.gelu(hsum, approximate=False) @ w2


def sel_cover(nc, nb):
    cs = jnp.arange(nc)[:, None] * CMP_STRIDE
    bs = jnp.arange(nb)[None, :] * SEL_BLOCK
    cov = jnp.clip(jnp.minimum(cs + CMP_LEN, bs + SEL_BLOCK) - jnp.maximum(cs, bs), 0, None)
    return cov.astype(jnp.float32) / CMP_LEN


def nsa_block(q, qpos, gate, kc, vc, ks, vs, kw, vw, wpos):
    Tq = q.shape[0]
    nc = kc.shape[0]
    cend = jnp.arange(nc) * CMP_STRIDE + CMP_LEN - 1
    s_c = jnp.einsum('tghd,cgd->tghc', q, kc) * ATTN_SCALE
    p_c = masked_softmax(s_c, (cend[None, :] <= qpos[:, None])[:, None, None, :])
    o_c = jnp.einsum('tghc,cgd->tghd', p_c, vc)
    L = ks.shape[0]
    nb = -(-L // SEL_BLOCK)
    padr = ((0, nb * SEL_BLOCK - L), (0, 0), (0, 0))
    ksb = jnp.pad(ks, padr).reshape(nb, SEL_BLOCK, NSA_KV, HEAD_DIM).transpose(2, 0, 1, 3)
    vsb = jnp.pad(vs, padr).reshape(nb, SEL_BLOCK, NSA_KV, HEAD_DIM).transpose(2, 0, 1, 3)
    p_blk = jnp.einsum('tghc,cn->tgn', p_c, sel_cover(nc, nb))
    cur = qpos // SEL_BLOCK
    j = jnp.arange(nb)[None, :]
    causal_blk = j <= cur[:, None]
    forced = (j == 0) | (j == cur[:, None]) | (j == cur[:, None] - 1)
    score = jnp.where(forced[:, None, :], FORCED_SCORE, jnp.where(causal_blk[:, None, :], p_blk, -jnp.inf))
    _, idx = lax.top_k(score, min(N_SELECT, nb))
    valid = jnp.take_along_axis(jnp.broadcast_to(causal_blk[:, None, :], score.shape), idx, axis=-1)
    g_ix = jnp.arange(NSA_KV)[None, :, None]
    kg = ksb[g_ix, idx]
    vg = vsb[g_ix, idx]
    n_sel = idx.shape[-1] * SEL_BLOCK
    spos = idx[..., None] * SEL_BLOCK + jnp.arange(SEL_BLOCK)
    smask = (valid[..., None] & (spos <= qpos[:, None, None, None])).reshape(Tq, NSA_KV, 1, n_sel)
    s_s = jnp.einsum('tghd,tgsnd->tghsn', q, kg).reshape(Tq, NSA_KV, NSA_HPG, n_sel) * ATTN_SCALE
    p_s = masked_softmax(s_s, smask)
    o_s = jnp.einsum('tghm,tgmd->tghd', p_s, vg.reshape(Tq, NSA_KV, n_sel, HEAD_DIM))
    dist = qpos[:, None] - wpos[None, :]
    wmask = (wpos[None, :] >= 0) & (dist >= 0) & (dist < WINDOW)
    s_w = jnp.einsum('tghd,ngd->tghn', q, kw) * ATTN_SCALE
    p_w = masked_softmax(s_w, wmask[:, None, None, :])
    o_w = jnp.einsum('tghn,ngd->tghd', p_w, vw)
    return gate[..., 0:1] * o_c + gate[..., 1:2] * o_s + gate[..., 2:3] * o_w


def nsa_prompt(q, gate, kc, vc, ks, vs, kw, vw):
    B, T = q.shape[:2]
    nq = T // Q_BLOCK
    pad = ((0, 0), (WINDOW, 0), (0, 0), (0, 0))
    kw_pad = jnp.pad(kw, pad)
    vw_pad = jnp.pad(vw, pad)
    qb = q.reshape((B * nq, Q_BLOCK) + q.shape[2:])
    gb = gate.reshape((B * nq, Q_BLOCK) + gate.shape[2:])

    def step(args):
        qi, gi, i = args
        b = i // nq
        start = (i % nq) * Q_BLOCK
        qpos = start + jnp.arange(Q_BLOCK)
        kwi = lax.dynamic_slice_in_dim(kw_pad[b], start, WINDOW + Q_BLOCK, axis=0)
        vwi = lax.dynamic_slice_in_dim(vw_pad[b], start, WINDOW + Q_BLOCK, axis=0)
        wpos = start - WINDOW + jnp.arange(WINDOW + Q_BLOCK)
        return nsa_block(qi, qpos, gi, kc[b], vc[b], ks[b], vs[b], kwi, vwi, wpos)

    o = lax.map(step, (qb, gb, jnp.arange(B * nq)))
    return o.reshape(B, T, NSA_W)


def fox_attend(q, c_q, qpos, k, v, c_k):
    kpos = jnp.arange(k.shape[1])
    s = jnp.einsum('bqhd,bkhd->bhqk', q, k).astype(jnp.float32) * ATTN_SCALE
    s = s + jnp.swapaxes(c_q, 1, 2)[..., :, None] - jnp.swapaxes(c_k, 1, 2)[..., None, :]
    p = masked_softmax(s, kpos[None, :] <= qpos[:, None])
    return jnp.einsum('bhqk,bkhd->bqhd', p, v)


def mixer_prompt(h, w_in, fox_fb, cmp_pe, cmp_w1, cmp_w2):
    B, T, _ = h.shape
    pos = jnp.arange(T)
    fq, fk, fv, logf, nq, nkv, gate = project(h, w_in, fox_fb)
    c = jnp.cumsum(logf, axis=1)
    nqb = T // Q_BLOCK

    def fox_step(args):
        qi, ci, jb = args
        return fox_attend(qi, ci, jb * Q_BLOCK + jnp.arange(Q_BLOCK), fk, fv, c)

    of = lax.map(fox_step, (fq.reshape(B, nqb, Q_BLOCK, FOX_HEADS, HEAD_DIM).swapaxes(0, 1),
                            c.reshape(B, nqb, Q_BLOCK, FOX_HEADS).swapaxes(0, 1), jnp.arange(nqb)))
    of = of.swapaxes(0, 1).reshape(B, T, FOX_W)
    q, rows = nsa_rows(nq, nkv, pos)
    kc = compress(rows[:, :, 0], cmp_pe[0], cmp_w1[0], cmp_w2[0])
    vc = compress(rows[:, :, 1], cmp_pe[1], cmp_w1[1], cmp_w2[1])
    on = nsa_prompt(q, gate, kc, vc, rows[:, :, 2], rows[:, :, 3], rows[:, :, 4], rows[:, :, 5])
    win_rows = rows[:, T - min(WINDOW, T):, 4:]
    return of, on, jnp.stack([fk, fv], axis=2), logf.astype(h.dtype), rows[:, :, :4], win_rows


def mixer_sample(h, fkv_past, logf_past, nkv_past, win_buf, w_in, fox_fb, cmp_pe, cmp_w1, cmp_w2):
    Bd, S, _ = h.shape
    P = fkv_past.shape[1]
    WB = win_buf.shape[1]
    pos = P + jnp.arange(S)
    fq, fk, fv, logf, nq, nkv, gate = project(h, w_in, fox_fb)
    fkv_new = jnp.stack([fk, fv], axis=2)
    fkv_all = jnp.concatenate([fkv_past, fkv_new.astype(fkv_past.dtype)], axis=1)
    c_all = jnp.cumsum(jnp.concatenate([logf_past.astype(jnp.float32), logf], axis=1), axis=1)
    of = fox_attend(fq, c_all[:, P:], pos, fkv_all[:, :, 0], fkv_all[:, :, 1], c_all).reshape(Bd, S, FOX_W)
    q, rows = nsa_rows(nq, nkv, pos)
    nkv_all = jnp.concatenate([nkv_past, rows[:, :, :4].astype(nkv_past.dtype)], axis=1)
    kc = compress(nkv_all[:, :, 0], cmp_pe[0], cmp_w1[0], cmp_w2[0])
    vc = compress(nkv_all[:, :, 1], cmp_pe[1], cmp_w1[1], cmp_w2[1])
    win_all = jnp.concatenate([win_buf, rows[:, :, 4:].astype(win_buf.dtype)], axis=1)
    wpos = jnp.concatenate([P - WB + jnp.arange(WB), pos])
    f = jax.vmap(nsa_block, in_axes=(0, None, 0, 0, 0, 0, 0, 0, 0, None))
    on = f(q, pos, gate, kc, vc, nkv_all[:, :, 2], nkv_all[:, :, 3], win_all[:, :, 0], win_all[:, :, 1], wpos)
    return of, on.reshape(Bd, S, NSA_W), fkv_new, logf.astype(h.dtype), rows[:, :, :4], win_all[:, S:]


def merge(x, of, on, g_fox, g_nsa, w_out):
    y = jnp.concatenate([rms_norm(of, g_fox), rms_norm(on, g_nsa)], axis=-1).astype(x.dtype)
    return x + (y @ w_out).astype(x.dtype)


def peer_ffn(h, wq, subkeys, u, v):
    N, D = h.shape
    q = (h @ wq).reshape(N, PEER_HEADS, 2, PEER_QDIM // 2)
    s = jnp.einsum('nhpd,pkd->nhpk', q, subkeys).astype(jnp.float32)
    sv, si = lax.top_k(s, PEER_TOPK)
    cand = sv[:, :, 0, :, None] + sv[:, :, 1, None, :]
    cidx = si[:, :, 0, :, None] * PEER_KEYS + si[:, :, 1, None, :]
    fv, fi = lax.top_k(cand.reshape(N, PEER_HEADS, PEER_TOPK * PEER_TOPK), PEER_TOPK)
    eidx = jnp.take_along_axis(cidx.reshape(N, PEER_HEADS, PEER_TOPK * PEER_TOPK), fi, axis=-1)
    g = jax.nn.softmax(fv, axis=-1)
    pad = (-N) % PEER_TOK_BLOCK
    nblk = (N + pad) // PEER_TOK_BLOCK
    hp = jnp.pad(h, ((0, pad), (0, 0))).reshape(nblk, PEER_TOK_BLOCK, D)
    ep = jnp.pad(eidx, ((0, pad), (0, 0), (0, 0))).reshape(nblk, PEER_TOK_BLOCK, PEER_HEADS, PEER_TOPK)
    gp = jnp.pad(g, ((0, pad), (0, 0), (0, 0))).reshape(nblk, PEER_TOK_BLOCK, PEER_HEADS, PEER_TOPK)

    def blk(args):
        hb, eb, gb = args
        a = jax.nn.gelu(jnp.einsum('td,thkd->thk', hb, u[eb]).astype(jnp.float32), approximate=False)
        return jnp.einsum('thk,thkd->td', (gb * a).astype(h.dtype), v[eb])

    return lax.map(blk, (hp, ep, gp)).reshape(nblk * PEER_TOK_BLOCK, D)[:N]


def channel(x, g, wq, subkeys, u, v):
    h = rms_norm(x, g)
    out = peer_ffn(h.reshape(-1, x.shape[-1]), wq, subkeys, u, v).reshape(x.shape)
    return x + out.astype(x.dtype)


def setup_inputs(seed: int = 0) -> dict:
    key = jax.random.key(seed)
    ks = jax.random.split(key, 24)
    f32 = jnp.float32
    nrm = jax.random.normal
    n_pages = PAST_LEN // PAGE_SIZE
    n_used = DEC_BATCH * n_pages
    n_pool = n_used + max(1, n_used // 4)
    win_buf = min(WINDOW, PAST_LEN)
    x_prompt = nrm(ks[0], (BATCH, SEQ, D_MODEL), f32)
    x_sample = nrm(ks[1], (DEC_BATCH, DEC_SEQ, D_MODEL), f32)
    cache_fox_kv = nrm(ks[2], (DEPTH, n_pool, PAGE_SIZE, 2, FOX_HEADS, HEAD_DIM), f32)
    cache_fox_logf = jax.nn.log_sigmoid(FORGET_BIAS + nrm(ks[3], (DEPTH, n_pool, PAGE_SIZE, FOX_HEADS), f32))
    cache_nsa_kv = nrm(ks[4], (DEPTH, n_pool, PAGE_SIZE, 4, NSA_KV, HEAD_DIM), f32)
    state_win_kv = nrm(ks[5], (DEPTH, DEC_BATCH, win_buf, 2, NSA_KV, HEAD_DIM), f32)
    page_table = jax.random.permutation(ks[6], n_pool)[:n_used].reshape(DEC_BATCH, n_pages).astype(jnp.int32)
    norm_mix = 1.0 + 0.02 * nrm(ks[7], (DEPTH, D_MODEL), f32)
    w_in = nrm(ks[8], (DEPTH, D_MODEL, IN_COLS), f32) * D_MODEL ** -0.5
    fox_fb = FORGET_BIAS + 0.1 * nrm(ks[9], (DEPTH, FOX_HEADS), f32)
    cmp_pe = 0.1 * nrm(ks[10], (DEPTH, 2, CMP_LEN, HEAD_DIM), f32)
    cmp_w1 = nrm(ks[11], (DEPTH, 2, CMP_LEN * HEAD_DIM, HEAD_DIM), f32) * (CMP_LEN * HEAD_DIM) ** -0.5
    cmp_w2 = nrm(ks[12], (DEPTH, 2, HEAD_DIM, HEAD_DIM), f32) * (2.0 / HEAD_DIM) ** 0.5
    norm_fox = 1.0 + 0.02 * nrm(ks[13], (DEPTH, FOX_W), f32)
    norm_nsa = 1.0 + 0.02 * nrm(ks[14], (DEPTH, NSA_W), f32)
    w_out = nrm(ks[15], (DEPTH, MIX_W, D_MODEL), f32) * MIX_W ** -0.5
    norm_ffn = 1.0 + 0.02 * nrm(ks[16], (DEPTH, D_MODEL), f32)
    peer_wq = nrm(ks[17], (DEPTH, D_MODEL, PEER_HEADS * PEER_QDIM), f32) * D_MODEL ** -0.5
    peer_subkeys = nrm(ks[18], (DEPTH, 2, PEER_KEYS, PEER_QDIM // 2), f32) * (PEER_QDIM // 2) ** -0.5
    peer_u = nrm(ks[19], (DEPTH, PEER_EXPERTS, D_MODEL), f32) * D_MODEL ** -0.5
    peer_v = nrm(ks[20], (DEPTH, PEER_EXPERTS, D_MODEL), f32) * PEER_HEADS ** -0.5
    norm_final = 1.0 + 0.02 * nrm(ks[21], (D_MODEL,), f32)
    return {'x_prompt': x_prompt, 'x_sample': x_sample, 'cache_fox_kv': cache_fox_kv,
            'cache_fox_logf': cache_fox_logf, 'cache_nsa_kv': cache_nsa_kv, 'state_win_kv': state_win_kv,
            'page_table': page_table, 'norm_mix': norm_mix, 'w_in': w_in, 'fox_fb': fox_fb,
            'cmp_pe': cmp_pe, 'cmp_w1': cmp_w1, 'cmp_w2': cmp_w2, 'norm_fox': norm_fox, 'norm_nsa': norm_nsa,
            'w_out': w_out, 'norm_ffn': norm_ffn, 'peer_wq': peer_wq, 'peer_subkeys': peer_subkeys,
            'peer_u': peer_u, 'peer_v': peer_v, 'norm_final': norm_final}


def reference(x_prompt, x_sample, cache_fox_kv, cache_fox_logf, cache_nsa_kv, state_win_kv, page_table,
              norm_mix, w_in, fox_fb, cmp_pe, cmp_w1, cmp_w2, norm_fox, norm_nsa, w_out,
              norm_ffn, peer_wq, peer_subkeys, peer_u, peer_v, norm_final):
    n_seq, n_pages = page_table.shape
    past_len = n_pages * cache_fox_kv.shape[2]
    xp, xs = x_prompt, x_sample
    fkv_p_l, lf_p_l, nkv_p_l, win_p_l = [], [], [], []
    fkv_s_l, lf_s_l, nkv_s_l, win_s_l = [], [], [], []
    for l in range(DEPTH):
        hp = rms_norm(xp, norm_mix[l])
        of_p, on_p, fkv_p, lf_p, nkv_p, win_p = mixer_prompt(hp, w_in[l], fox_fb[l], cmp_pe[l], cmp_w1[l], cmp_w2[l])
        hs = rms_norm(xs, norm_mix[l])
        fkv_past = cache_fox_kv[l, page_table].reshape((n_seq, past_len) + cache_fox_kv.shape[3:])
        lf_past = cache_fox_logf[l, page_table].reshape(n_seq, past_len, FOX_HEADS)
        nkv_past = cache_nsa_kv[l, page_table].reshape((n_seq, past_len) + cache_nsa_kv.shape[3:])
        of_s, on_s, fkv_s, lf_s, nkv_s, win_s = mixer_sample(hs, fkv_past, lf_past, nkv_past, state_win_kv[l],
                                                             w_in[l], fox_fb[l], cmp_pe[l], cmp_w1[l], cmp_w2[l])
        xp = merge(xp, of_p, on_p, norm_fox[l], norm_nsa[l], w_out[l])
        xs = merge(xs, of_s, on_s, norm_fox[l], norm_nsa[l], w_out[l])
        xp = channel(xp, norm_ffn[l], peer_wq[l], peer_subkeys[l], peer_u[l], peer_v[l])
        xs = channel(xs, norm_ffn[l], peer_wq[l], peer_subkeys[l], peer_u[l], peer_v[l])
        fkv_p_l.append(fkv_p)
        lf_p_l.append(lf_p)
        nkv_p_l.append(nkv_p)
        win_p_l.append(win_p)
        fkv_s_l.append(fkv_s)
        lf_s_l.append(lf_s)
        nkv_s_l.append(nkv_s)
        win_s_l.append(win_s)
    y_prompt = rms_norm(xp, norm_final)
    y_sample = rms_norm(xs, norm_final)
    return (y_prompt, y_sample,
            jnp.stack(fkv_p_l), jnp.stack(lf_p_l), jnp.stack(nkv_p_l), jnp.stack(win_p_l),
            jnp.stack(fkv_s_l), jnp.stack(lf_s_l), jnp.stack(nkv_s_l), jnp.stack(win_s_l))
```

```python
import functools

import jax
import jax.numpy as jnp
import numpy as np
from jax import lax
from jax.experimental import pallas as pl
from jax.experimental.pallas import tpu as pltpu

HEAD_DIM = 64
FOX_HEADS = 8
NSA_HEADS = 8
NSA_KV = 2
NSA_HPG = NSA_HEADS // NSA_KV
FOX_W = FOX_HEADS * HEAD_DIM
NSA_W = NSA_HEADS * HEAD_DIM
NSA_KV_W = NSA_KV * HEAD_DIM
CMP_LEN = 32
CMP_STRIDE = 16
SEL_BLOCK = 64
N_SELECT = 16
WINDOW = 512
FORCED_SCORE = 1e4
ROPE_THETA = 10000.0
ATTN_SCALE = HEAD_DIM ** -0.5
PEER_KEYS = 128
PEER_HEADS = 8
PEER_QDIM = 256
PEER_TOPK = 16
RMS_EPS = 1e-6

LANES = 128
VMEM_LIMIT = 48 * 1024 * 1024
BF16 = jnp.bfloat16
F32 = jnp.float32
NEG_INF = float("-inf")


def _cparams(*sem):
    return pltpu.CompilerParams(dimension_semantics=sem, vmem_limit_bytes=VMEM_LIMIT)


def _gelu(x):
    return 0.5 * x * (1.0 + lax.erf(x * np.float32(np.sqrt(0.5))))


def _extract_topk(x, k):
    n_rows = x.shape[0]
    iota = lax.broadcasted_iota(jnp.int32, x.shape, 0)
    rank = jnp.full(x.shape, k, jnp.int32)
    vals, idxs = [], []
    for r in range(k):
        m = jnp.max(x, axis=0, keepdims=True)
        i = jnp.min(jnp.where(x == m, iota, n_rows), axis=0, keepdims=True)
        hit = iota == i
        vals.append(m)
        idxs.append(i)
        rank = jnp.where(hit, r, rank)
        x = jnp.where(hit, NEG_INF, x)
    return vals, idxs, rank


_PEER_CAND = [(a, b) for a in range(PEER_TOPK) for b in range(PEER_TOPK) if (a + 1) * (b + 1) <= PEER_TOPK]


def _peer_route_kernel(xt_ref, g_ref, wqt_ref, sk_ref, ht_ref, r1_ref, cnt_ref, a_ref, b_ref):
    x = xt_ref[...]
    ms = jnp.mean(x * x, axis=0, keepdims=True)
    h = (x * lax.rsqrt(ms + RMS_EPS)) * g_ref[...]
    hb = h.astype(BF16)
    ht_ref[...] = hb
    q = jnp.dot(wqt_ref[...], hb, preferred_element_type=F32)
    half = PEER_QDIM // 2
    for hd in range(PEER_HEADS):
        q0 = q[hd * PEER_QDIM: hd * PEER_QDIM + half].astype(BF16)
        q1 = q[hd * PEER_QDIM + half: (hd + 1) * PEER_QDIM].astype(BF16)
        s0 = jnp.dot(sk_ref[0], q0, preferred_element_type=F32)
        s1 = jnp.dot(sk_ref[1], q1, preferred_element_type=F32)
        v0, i0, _ = _extract_topk(s0, PEER_TOPK)
        v1, _, rank1 = _extract_topk(s1, PEER_TOPK)
        cand = jnp.concatenate([v0[a] + v1[b] for a, b in _PEER_CAND], axis=0)
        fv, _, crank = _extract_topk(cand, PEER_TOPK)
        sel = crank < PEER_TOPK
        z = fv[0] - fv[0]
        for r in range(PEER_TOPK):
            z = z + jnp.exp(fv[r] - fv[0])
        inv_z = 1.0 / z
        row = 0
        iota = lax.broadcasted_iota(jnp.int32, s0.shape, 0)
        cnt_full = jnp.zeros(s0.shape, F32)
        for a in range(PEER_TOPK):
            nb = PEER_TOPK // (a + 1)
            cnt_a = jnp.sum(sel[row:row + nb].astype(F32), axis=0, keepdims=True)
            row += nb
            cnt_full = jnp.where(iota == i0[a], cnt_a, cnt_full)
        sl = slice(hd * PEER_KEYS, (hd + 1) * PEER_KEYS)
        r1_ref[sl, :] = rank1.astype(F32)
        cnt_ref[sl, :] = cnt_full
        a_ref[sl, :] = jnp.exp(s0 - v0[0])
        b_ref[sl, :] = jnp.exp(s1 - v1[0]) * inv_z


def _peer_dense_kernel(xt_ref, ht_ref, r1_ref, cnt_ref, a_ref, b_ref, u_ref, vt_ref, o_ref, ga_ref, *, rows_per_step):
    c = pl.program_id(1)

    @pl.when(c == 0)
    def _():
        o_ref[...] = xt_ref[...]

    def body(j, carry):
        off = pl.multiple_of(j * PEER_KEYS, PEER_KEYS)
        pre = jnp.dot(u_ref[pl.ds(off, PEER_KEYS), :], ht_ref[...], preferred_element_type=F32)
        act = _gelu(pre)
        i1 = c * rows_per_step + j
        g = jnp.zeros(pre.shape, F32)
        for hd in range(PEER_HEADS):
            sl = slice(hd * PEER_KEYS, (hd + 1) * PEER_KEYS)
            cnt_row = cnt_ref[pl.ds(hd * PEER_KEYS + i1, 1), :]
            a_row = a_ref[pl.ds(hd * PEER_KEYS + i1, 1), :]
            g = g + jnp.where(r1_ref[sl, :] < cnt_row, b_ref[sl, :] * a_row, 0.0)
        ga_ref[pl.ds(off, PEER_KEYS), :] = (g * act).astype(BF16)
        return carry

    lax.fori_loop(0, rows_per_step, body, 0)
    o_ref[...] += jnp.dot(vt_ref[...], ga_ref[...], preferred_element_type=F32)


def _peer_channel_t(xt, g_ffn, wqt, sk, u_bf, vt_bf, tn, rows_per_step=4):
    d, n = xt.shape
    nt = n // tn
    hk = PEER_HEADS * PEER_KEYS
    tok = lambda i: (0, i)
    full = lambda i: (0, 0)
    ht, r1, cnt, a, b = pl.pallas_call(
        _peer_route_kernel,
        grid=(nt,),
        in_specs=[pl.BlockSpec((d, tn), tok), pl.BlockSpec((d, 1), full),
                  pl.BlockSpec(wqt.shape, full), pl.BlockSpec(sk.shape, lambda i: (0, 0, 0))],
        out_specs=[pl.BlockSpec((d, tn), tok)] + [pl.BlockSpec((hk, tn), tok)] * 4,
        out_shape=[jax.ShapeDtypeStruct((d, n), BF16)] + [jax.ShapeDtypeStruct((hk, n), F32)] * 4,
        compiler_params=_cparams("parallel"),
    )(xt, g_ffn.reshape(d, 1), wqt, sk)
    ec = rows_per_step * PEER_KEYS
    n_chunks = u_bf.shape[0] // ec
    tok2 = lambda i, c: (0, i)
    return pl.pallas_call(
        functools.partial(_peer_dense_kernel, rows_per_step=rows_per_step),
        grid=(nt, n_chunks),
        in_specs=[pl.BlockSpec((d, tn), tok2), pl.BlockSpec((d, tn), tok2)]
                 + [pl.BlockSpec((hk, tn), tok2)] * 4
                 + [pl.BlockSpec((ec, d), lambda i, c: (c, 0)), pl.BlockSpec((d, ec), lambda i, c: (0, c))],
        out_specs=pl.BlockSpec((d, tn), tok2),
        out_shape=jax.ShapeDtypeStruct((d, n), F32),
        scratch_shapes=[pltpu.VMEM((ec, tn), BF16)],
        compiler_params=_cparams("parallel", "arbitrary"),
    )(xt, ht, r1, cnt, a, b, u_bf, vt_bf)


_C_FQ = 0
_C_FKV = _C_FQ + FOX_W
_C_MISC = _C_FKV + 2 * FOX_W
_C_NQ = _C_MISC + LANES
_C_NQR = _C_NQ + NSA_W
_C_NKV = _C_NQR + NSA_W
_C_NKR = _C_NKV + 6 * NSA_KV_W
_C_END = _C_NKR + 3 * NSA_KV_W
N_GATES = 3 * NSA_HEADS


def _rot_cols(w):
    d, c = w.shape
    half = HEAD_DIM // 2
    w4 = w.reshape(d, c // HEAD_DIM, 2, half)
    return jnp.stack([-w4[:, :, 1], w4[:, :, 0]], axis=2).reshape(d, c)


_NSA_PAIR_ORDER = [g * NSA_HPG + j for j in range(NSA_HPG) for g in range(NSA_KV)]


def _pair_perm(n_per_head):
    return np.concatenate([np.arange(h * n_per_head, (h + 1) * n_per_head) for h in _NSA_PAIR_ORDER])


def _prep_w_in(w_in, fox_fb):
    d = w_in.shape[0]
    o = 0
    fq = w_in[:, o:o + FOX_W]; o += FOX_W
    fkv = w_in[:, o:o + 2 * FOX_W]; o += 2 * FOX_W
    ff = w_in[:, o:o + FOX_HEADS]; o += FOX_HEADS
    nq = w_in[:, o:o + NSA_W]; o += NSA_W
    nkv = w_in[:, o:o + 6 * NSA_KV_W]; o += 6 * NSA_KV_W
    ng = w_in[:, o:o + N_GATES]
    misc = jnp.concatenate([ff, ng, jnp.zeros((d, LANES - FOX_HEADS - N_GATES), w_in.dtype)], axis=1)
    nq = nq[:, _pair_perm(HEAD_DIM)]
    nk = nkv.reshape(d, 3, 2, NSA_KV_W)[:, :, 0].reshape(d, 3 * NSA_KV_W)
    w_all = jnp.concatenate([fq, fkv, misc, nq, _rot_cols(nq), nkv, _rot_cols(nk)], axis=1).astype(BF16)
    fb = jnp.concatenate([fox_fb.astype(F32), jnp.zeros((LANES - FOX_HEADS,), F32)]).reshape(1, LANES)
    return w_all, fb


def _rope_tables(pos):
    half = HEAD_DIM // 2
    inv = ROPE_THETA ** (-jnp.arange(half, dtype=F32) / half)
    ang = pos.astype(F32)[:, None] * inv[None, :]
    reps = LANES // half
    return jnp.tile(jnp.cos(ang), (1, reps)), jnp.tile(jnp.sin(ang), (1, reps))


def _proj_kernel(x_ref, g_ref, w_ref, fb_ref, cos_ref, sin_ref,
                 fq_ref, fkv_ref, fkvb_ref, misc_ref, nq_ref, nkv_ref, nkvb_ref):
    x = x_ref[...]
    ms = jnp.mean(x * x, axis=-1, keepdims=True)
    hb = ((x * lax.rsqrt(ms + RMS_EPS)) * g_ref[...]).astype(BF16)

    def mm(c0, c1):
        return jnp.dot(hb, w_ref[:, c0:c1], preferred_element_type=F32)

    fq_ref[...] = (mm(_C_FQ, _C_FKV) * ATTN_SCALE).astype(BF16)
    fkv = mm(_C_FKV, _C_MISC)
    fkv_ref[...] = fkv
    fkvb_ref[...] = fkv.astype(BF16)
    z = mm(_C_MISC, _C_NQ) + fb_ref[...]
    lane = lax.broadcasted_iota(jnp.int32, z.shape, 1)
    log_sig = jnp.minimum(z, 0.0) - jnp.log1p(jnp.exp(-jnp.abs(z)))
    misc_ref[...] = jnp.where(lane < FOX_HEADS, log_sig, jax.nn.sigmoid(z))
    cos = cos_ref[...]
    sin = sin_ref[...]
    reps = NSA_W // LANES
    cos_q = jnp.concatenate([cos] * reps, axis=1)
    sin_q = jnp.concatenate([sin] * reps, axis=1)
    nq = mm(_C_NQ, _C_NQR) * cos_q + mm(_C_NQR, _C_NKV) * sin_q
    nq_ref[...] = (nq * ATTN_SCALE).astype(BF16)
    for s in range(6):
        c0 = _C_NKV + s * NSA_KV_W
        v = mm(c0, c0 + NSA_KV_W)
        if s % 2 == 0:
            r0 = _C_NKR + (s // 2) * NSA_KV_W
            v = v * cos + mm(r0, r0 + NSA_KV_W) * sin
        nkv_ref[:, s * NSA_KV_W:(s + 1) * NSA_KV_W] = v
        nkvb_ref[:, s * NSA_KV_W:(s + 1) * NSA_KV_W] = v.astype(BF16)


def _project(x2, g_mix, w_all, fb, cos, sin, tm):
    n, d = x2.shape
    tok = lambda i: (i, 0)
    full = lambda i: (0, 0)
    widths = [(FOX_W, BF16), (2 * FOX_W, F32), (2 * FOX_W, BF16), (LANES, F32), (NSA_W, BF16),
              (6 * NSA_KV_W, F32), (6 * NSA_KV_W, BF16)]
    return pl.pallas_call(
        _proj_kernel,
        grid=(n // tm,),
        in_specs=[pl.BlockSpec((tm, d), tok), pl.BlockSpec((1, d), full), pl.BlockSpec(w_all.shape, full),
                  pl.BlockSpec((1, LANES), full), pl.BlockSpec((tm, LANES), tok), pl.BlockSpec((tm, LANES), tok)],
        out_specs=[pl.BlockSpec((tm, w), tok) for w, _ in widths],
        out_shape=[jax.ShapeDtypeStruct((n, w), dt) for w, dt in widths],
        compiler_params=_cparams("parallel"),
    )(x2, g_mix.reshape(1, d), w_all, fb, cos, sin)


def _merge_kernel(x_ref, of_ref, on_ref, gf_ref, gn_ref, wf_ref, wn_ref, o_ref):
    def norm(v, g):
        v = v.astype(F32)
        ms = jnp.mean(v * v, axis=-1, keepdims=True)
        return ((v * lax.rsqrt(ms + RMS_EPS)) * g).astype(BF16)

    yf = norm(of_ref[...], gf_ref[...])
    yn = norm(on_ref[...], gn_ref[...])
    o_ref[...] = (x_ref[...] + jnp.dot(yf, wf_ref[...], preferred_element_type=F32)
                  + jnp.dot(yn, wn_ref[...], preferred_element_type=F32))


def _merge(x2, of, on, g_fox, g_nsa, w_f, w_n, tm):
    n, d = x2.shape
    tok = lambda i: (i, 0)
    full = lambda i: (0, 0)
    return pl.pallas_call(
        _merge_kernel,
        grid=(n // tm,),
        in_specs=[pl.BlockSpec((tm, d), tok), pl.BlockSpec((tm, FOX_W), tok), pl.BlockSpec((tm, NSA_W), tok),
                  pl.BlockSpec((1, FOX_W), full), pl.BlockSpec((1, NSA_W), full),
                  pl.BlockSpec((FOX_W, d), full), pl.BlockSpec((NSA_W, d), full)],
        out_specs=pl.BlockSpec((tm, d), tok),
        out_shape=jax.ShapeDtypeStruct((n, d), F32),
        compiler_params=_cparams("parallel"),
    )(x2, of, on, g_fox.reshape(1, -1), g_nsa.reshape(1, -1), w_f, w_n)


_NT = (((1,), (1,)), ((), ()))


def _flash_init(m_ref, l_ref, acc_ref):
    m_ref[...] = jnp.full(m_ref.shape, NEG_INF, F32)
    l_ref[...] = jnp.zeros(l_ref.shape, F32)
    acc_ref[...] = jnp.zeros(acc_ref.shape, F32)


def _flash_steps(qh, load_k, load_v, bias_fn, lo, hi, tk, m_ref, l_ref, acc_ref):
    def body(kt, carry):
        ks = pl.multiple_of(kt * tk, tk)
        s = lax.dot_general(qh, load_k(ks), _NT, preferred_element_type=F32) + bias_fn(ks)
        m_prev = m_ref[...]
        m_new = jnp.maximum(m_prev, jnp.max(s, axis=1, keepdims=True))
        m_safe = jnp.where(m_new == NEG_INF, 0.0, m_new)
        p = jnp.exp(s - m_safe)
        alpha = jnp.exp(m_prev - m_safe)
        l_ref[...] = alpha * l_ref[...] + jnp.sum(p, axis=1, keepdims=True)
        acc_ref[...] = alpha * acc_ref[...] + jnp.dot(p.astype(BF16), load_v(ks), preferred_element_type=F32)
        m_ref[...] = m_new
        return carry

    lax.fori_loop(lo, hi, body, 0)


def _flash_finish(l_ref, acc_ref):
    return acc_ref[...] / jnp.maximum(l_ref[...], 1e-30)


def _half_mask(shape, hh):
    lane = lax.broadcasted_iota(jnp.int32, shape, 1)
    return (lane >= HEAD_DIM) if hh else (lane < HEAD_DIM)


def _causal_bias(bias, qstart, ks):
    qpos = qstart + lax.broadcasted_iota(jnp.int32, bias.shape, 0)
    kpos = ks + lax.broadcasted_iota(jnp.int32, bias.shape, 1)
    return jnp.where(kpos <= qpos, bias, NEG_INF)


def _fox_prompt_kernel(q_ref, k_ref, v_ref, cq_ref, ck_ref, o_ref, m_ref, l_ref, acc_ref, *, tq):
    j = pl.program_id(1)
    qi = pl.program_id(2)
    qstart = qi * tq
    qp = q_ref[0]
    out = jnp.zeros(qp.shape, F32)
    load_k = lambda ks: k_ref[0, pl.ds(ks, tq), :]
    load_v = lambda ks: v_ref[0, pl.ds(ks, tq), :]
    for hh in range(2):
        half = _half_mask(qp.shape, hh)
        qh = jnp.where(half, qp, jnp.zeros_like(qp))
        cq = cq_ref[0, 0, :, hh:hh + 1]
        bias = lambda ks: cq - ck_ref[0, pl.ds(2 * j + hh, 1), pl.ds(ks, tq)]
        bias_diag = lambda ks: _causal_bias(bias(ks), qstart, ks)
        _flash_init(m_ref, l_ref, acc_ref)
        _flash_steps(qh, load_k, load_v, bias, 0, qi, tq, m_ref, l_ref, acc_ref)
        _flash_steps(qh, load_k, load_v, bias_diag, qi, qi + 1, tq, m_ref, l_ref, acc_ref)
        out = jnp.where(half, _flash_finish(l_ref, acc_ref), out)
    o_ref[0] = out


def _fox_prompt(fq, fkvb, c, tq):
    b, t, _ = fq.shape
    n_pairs = FOX_W // LANES
    cq = c.reshape(b, t, n_pairs, 2).transpose(0, 2, 1, 3)
    ck = c.transpose(0, 2, 1)
    return pl.pallas_call(
        functools.partial(_fox_prompt_kernel, tq=tq),
        grid=(b, n_pairs, t // tq),
        in_specs=[pl.BlockSpec((1, tq, LANES), lambda bi, j, qi: (bi, qi, j)),
                  pl.BlockSpec((1, t, LANES), lambda bi, j, qi: (bi, 0, j)),
                  pl.BlockSpec((1, t, LANES), lambda bi, j, qi: (bi, 0, n_pairs + j)),
                  pl.BlockSpec((1, 1, tq, 2), lambda bi, j, qi: (bi, j, qi, 0)),
                  pl.BlockSpec((1, FOX_HEADS, t), lambda bi, j, qi: (bi, 0, 0))],
        out_specs=pl.BlockSpec((1, tq, LANES), lambda bi, j, qi: (bi, qi, j)),
        out_shape=jax.ShapeDtypeStruct((b, t, FOX_W), F32),
        scratch_shapes=[pltpu.VMEM((tq, 1), F32), pltpu.VMEM((tq, 1), F32), pltpu.VMEM((tq, LANES), F32)],
        compiler_params=_cparams("parallel", "parallel", "arbitrary"),
    )(fq, fkvb, fkvb, cq, ck)


def _compress_kernel(x_ref, pe_ref, w1_ref, w2_ref, o_ref):
    x = (x_ref[0] + pe_ref[0]).astype(BF16)
    h = _gelu(jnp.dot(x, w1_ref[0], preferred_element_type=F32))
    o_ref[0] = jnp.dot(h.astype(BF16), w2_ref[0], preferred_element_type=F32)


def _compress(xw, pe, w1, w2, tr):
    _, r, kd = xw.shape
    return pl.pallas_call(
        _compress_kernel,
        grid=(2, r // tr),
        in_specs=[pl.BlockSpec((1, tr, kd), lambda s, i: (s, i, 0)),
                  pl.BlockSpec((1, 1, kd), lambda s, i: (s, 0, 0)),
                  pl.BlockSpec((1, kd, HEAD_DIM), lambda s, i: (s, 0, 0)),
                  pl.BlockSpec((1, HEAD_DIM, HEAD_DIM), lambda s, i: (s, 0, 0))],
        out_specs=pl.BlockSpec((1, tr, HEAD_DIM), lambda s, i: (s, i, 0)),
        out_shape=jax.ShapeDtypeStruct((2, r, HEAD_DIM), F32),
        compiler_params=_cparams("parallel", "parallel"),
    )(xw, pe.reshape(2, 1, kd), w1.astype(BF16), w2.astype(BF16))


def _cmp_windows(rows):
    two, b, length, g, d = rows.shape
    nch = length // CMP_STRIDE
    ch = rows[:, :, :nch * CMP_STRIDE].reshape(two, b, nch, CMP_STRIDE, g, d).transpose(0, 1, 2, 4, 3, 5)
    ch = ch.reshape(two, b, nch, g, CMP_STRIDE * d)
    reps = CMP_LEN // CMP_STRIDE
    nc = nch - reps + 1
    return jnp.concatenate([ch[:, :, r:r + nc] for r in range(reps)], axis=-1)


def _round_up(x, m):
    return -(-x // m) * m


def _compress_rows(rows, pe, w1, w2):
    xw = _cmp_windows(rows)
    two, b, nc, g, kd = xw.shape
    r = b * nc * g
    tr = min(512, _round_up(r, 8))
    rp = _round_up(r, tr)
    xw = jnp.pad(xw.reshape(2, r, kd), ((0, 0), (0, rp - r), (0, 0)))
    return _compress(xw, pe, w1, w2, tr)[:, :r].reshape(2, b, nc, g, HEAD_DIM)


def _sel_cover_t(nc, nb, ncp):
    cs = np.arange(ncp)[None, :] * CMP_STRIDE
    bs = np.arange(nb)[:, None] * SEL_BLOCK
    cov = np.clip(np.minimum(cs + CMP_LEN, bs + SEL_BLOCK) - np.maximum(cs, bs), 0, None)
    cov = np.where(np.arange(ncp)[None, :] < nc, cov, 0)
    return jnp.asarray(cov.astype(np.float32) / CMP_LEN, BF16)


def _nsa_prompt_kernel(q_ref, kv_ref, kc_ref, vc_ref, misc_ref, covt_ref, o_ref,
                       qh_ref, oc_ref, bias_ref, m_ref, l_ref, acc_ref, *, tq, nc, nb, n_sel):
    qi = pl.program_id(1)
    qstart = qi * tq
    ncp = kc_ref.shape[1]
    n_pairs = NSA_W // LANES
    load = lambda slot: (lambda ks: kv_ref[0, pl.ds(ks, tq), slot * LANES:(slot + 1) * LANES])

    cidx = lax.broadcasted_iota(jnp.int32, (tq, ncp), 1)
    qpos_c = qstart + lax.broadcasted_iota(jnp.int32, (tq, ncp), 0)
    cmask = (cidx * CMP_STRIDE + (CMP_LEN - 1) <= qpos_c) & (cidx < nc)
    psum = [jnp.zeros((tq, ncp), F32) for _ in range(NSA_KV)]
    for j in range(n_pairs):
        qp = q_ref[0, :, j * LANES:(j + 1) * LANES]
        for g in range(NSA_KV):
            h = j * NSA_KV + g
            qh = jnp.where(_half_mask(qp.shape, g), qp, jnp.zeros_like(qp))
            qh_ref[h] = qh
            s = lax.dot_general(qh, kc_ref[0], _NT, preferred_element_type=F32)
            s = jnp.where(cmask, s, NEG_INF)
            m = jnp.max(s, axis=1, keepdims=True)
            m = jnp.where(m == NEG_INF, 0.0, m)
            e = jnp.exp(s - m)
            p = e / jnp.maximum(jnp.sum(e, axis=1, keepdims=True), 1e-30)
            psum[g] = psum[g] + p
            oc_ref[h] = jnp.dot(p.astype(BF16), vc_ref[0], preferred_element_type=F32)

    for g in range(NSA_KV):
        hi = psum[g].astype(BF16)
        lo = (psum[g] - hi.astype(F32)).astype(BF16)
        pb = (lax.dot_general(covt_ref[...], hi, _NT, preferred_element_type=F32)
              + lax.dot_general(covt_ref[...], lo, _NT, preferred_element_type=F32))
        jj = lax.broadcasted_iota(jnp.int32, (nb, tq), 0)
        cur = (qstart + lax.broadcasted_iota(jnp.int32, (nb, tq), 1)) // SEL_BLOCK
        causal = jj <= cur
        forced = (jj == 0) | (jj == cur) | (jj == cur - 1)
        score = jnp.where(forced, FORCED_SCORE, jnp.where(causal, pb, NEG_INF))
        _, _, rank = _extract_topk(score, n_sel)
        sel_t = jnp.where((rank < n_sel) & causal, 1.0, 0.0)
        sel_t = jnp.concatenate([sel_t, jnp.zeros((LANES - nb, tq), F32)], axis=0)
        sel = sel_t.T.astype(BF16)

        def fill(kt, carry):
            ks = pl.multiple_of(kt * tq, tq)
            blk = (ks + lax.broadcasted_iota(jnp.int32, (LANES, tq), 1)) // SEL_BLOCK
            expand = jnp.where(lax.broadcasted_iota(jnp.int32, (LANES, tq), 0) == blk, 1.0, 0.0).astype(BF16)
            hit = jnp.dot(sel, expand, preferred_element_type=F32)
            bias = jnp.where(hit > 0.5, 0.0, NEG_INF)
            bias_ref[g, :, pl.ds(ks, tq)] = _causal_bias(bias, qstart, ks)
            return carry

        lax.fori_loop(0, qi + 1, fill, 0)

    def win_bias(ks):
        qpos = qstart + lax.broadcasted_iota(jnp.int32, (tq, tq), 0)
        kpos = ks + lax.broadcasted_iota(jnp.int32, (tq, tq), 1)
        dist = qpos - kpos
        return jnp.where((dist >= 0) & (dist < WINDOW), 0.0, NEG_INF)

    win_lo = jnp.maximum(qi - (-(-WINDOW // tq)), 0)
    for j in range(n_pairs):
        out = jnp.zeros((tq, LANES), F32)
        for g in range(NSA_KV):
            h = j * NSA_KV + g
            qh = qh_ref[h]
            _flash_init(m_ref, l_ref, acc_ref)
            _flash_steps(qh, load(2), load(3), lambda ks: bias_ref[g, :, pl.ds(ks, tq)], 0, qi + 1, tq,
                         m_ref, l_ref, acc_ref)
            o_s = _flash_finish(l_ref, acc_ref)
            _flash_init(m_ref, l_ref, acc_ref)
            _flash_steps(qh, load(4), load(5), win_bias, win_lo, qi + 1, tq, m_ref, l_ref, acc_ref)
            o_w = _flash_finish(l_ref, acc_ref)
            col = FOX_HEADS + (g * NSA_HPG + j) * 3
            gates = [misc_ref[0, :, col + r:col + r + 1] for r in range(3)]
            o = gates[0] * oc_ref[h] + gates[1] * o_s + gates[2] * o_w
            out = jnp.where(_half_mask(out.shape, g), o, out)
        o_ref[0, :, j * LANES:(j + 1) * LANES] = out


def _nsa_prompt(nq, nkvb, kc, vc, misc, tq):
    b, t, _ = nq.shape
    nc = kc.shape[1]
    ncp = _round_up(nc, LANES)
    nb = -(-t // SEL_BLOCK)
    assert nb <= LANES and t % tq == 0 and tq % SEL_BLOCK == 0
    pad = ((0, 0), (0, ncp - nc), (0, 0))
    kcp = jnp.pad(kc, pad).astype(BF16)
    vcp = jnp.pad(vc, pad).astype(BF16)
    covt = _sel_cover_t(nc, nb, ncp)
    kern = functools.partial(_nsa_prompt_kernel, tq=tq, nc=nc, nb=nb, n_sel=min(N_SELECT, nb))
    seq = lambda bi, qi: (bi, 0, 0)
    tile = lambda bi, qi: (bi, qi, 0)
    return pl.pallas_call(
        kern,
        grid=(b, t // tq),
        in_specs=[pl.BlockSpec((1, tq, NSA_W), tile), pl.BlockSpec((1, t, 6 * NSA_KV_W), seq),
                  pl.BlockSpec((1, ncp, LANES), seq), pl.BlockSpec((1, ncp, LANES), seq),
                  pl.BlockSpec((1, tq, LANES), tile), pl.BlockSpec((nb, ncp), lambda bi, qi: (0, 0))],
        out_specs=pl.BlockSpec((1, tq, NSA_W), tile),
        out_shape=jax.ShapeDtypeStruct((b, t, NSA_W), F32),
        scratch_shapes=[pltpu.VMEM((NSA_HEADS, tq, LANES), BF16), pltpu.VMEM((NSA_HEADS, tq, LANES), F32),
                        pltpu.VMEM((NSA_KV, tq, t), F32),
                        pltpu.VMEM((tq, 1), F32), pltpu.VMEM((tq, 1), F32), pltpu.VMEM((tq, LANES), F32)],
        compiler_params=_cparams("parallel", "arbitrary"),
    )(nq, nkvb, kcp, vcp, misc, covt)


def _tile(n, pref):
    t = min(pref, n)
    while n % t:
        t //= 2
    return t


def _mixer_prompt(x, g_mix, w_all, fb, cmp_pe, cmp_w1, cmp_w2):
    b, t, d = x.shape
    n = b * t
    cos, sin = _rope_tables(jnp.arange(t))
    cos = jnp.tile(cos, (b, 1))
    sin = jnp.tile(sin, (b, 1))
    fq, fkv, fkvb, misc, nq, nkv, nkvb = _project(x.reshape(n, d), g_mix, w_all, fb, cos, sin, _tile(n, 512))
    logf = misc[:, :FOX_HEADS].reshape(b, t, FOX_HEADS)
    c = jnp.cumsum(logf, axis=1)
    tq = _tile(t, 256)
    of = _fox_prompt(fq.reshape(b, t, FOX_W), fkvb.reshape(b, t, 2 * FOX_W), c, tq)
    rows = nkv.reshape(b, t, 6, NSA_KV, HEAD_DIM)
    kvc = _compress_rows(jnp.stack([rows[:, :, 0], rows[:, :, 1]]), cmp_pe, cmp_w1, cmp_w2)
    nc = kvc.shape[2]
    on = _nsa_prompt(nq.reshape(b, t, NSA_W), nkvb.reshape(b, t, 6 * NSA_KV_W),
                     kvc[0].reshape(b, nc, NSA_KV_W), kvc[1].reshape(b, nc, NSA_KV_W),
                     misc.reshape(b, t, LANES), tq)
    wb = min(WINDOW, t)
    return (of, on, fkv.reshape(b, t, 2, FOX_HEADS, HEAD_DIM), logf, rows[:, :, :4], rows[:, t - wb:, 4:])


def _masked_softmax(s, mask):
    s = jnp.where(mask, s.astype(F32), -jnp.inf)
    m = jnp.max(s, axis=-1, keepdims=True)
    m = jnp.where(jnp.isfinite(m), m, 0.0)
    e = jnp.where(mask, jnp.exp(s - m), 0.0)
    return e / jnp.maximum(jnp.sum(e, axis=-1, keepdims=True), 1e-30)


def _nsa_block_sample(q, qpos, gate, kc, vc, ks, vs, kw, vw, wpos):
    tq = q.shape[0]
    nc = kc.shape[0]
    cend = jnp.arange(nc) * CMP_STRIDE + CMP_LEN - 1
    s_c = jnp.einsum('tghd,cgd->tghc', q, kc)
    p_c = _masked_softmax(s_c, (cend[None, :] <= qpos[:, None])[:, None, None, :])
    o_c = jnp.einsum('tghc,cgd->tghd', p_c, vc)
    length = ks.shape[0]
    nb = -(-length // SEL_BLOCK)
    padr = ((0, nb * SEL_BLOCK - length), (0, 0), (0, 0))
    ksb = jnp.pad(ks, padr).reshape(nb, SEL_BLOCK, NSA_KV, HEAD_DIM).transpose(2, 0, 1, 3)
    vsb = jnp.pad(vs, padr).reshape(nb, SEL_BLOCK, NSA_KV, HEAD_DIM).transpose(2, 0, 1, 3)
    cs = jnp.arange(nc)[:, None] * CMP_STRIDE
    bs = jnp.arange(nb)[None, :] * SEL_BLOCK
    cover = jnp.clip(jnp.minimum(cs + CMP_LEN, bs + SEL_BLOCK) - jnp.maximum(cs, bs), 0, None).astype(F32) / CMP_LEN
    p_blk = jnp.einsum('tghc,cn->tgn', p_c, cover)
    cur = qpos // SEL_BLOCK
    j = jnp.arange(nb)[None, :]
    causal_blk = j <= cur[:, None]
    forced = (j == 0) | (j == cur[:, None]) | (j == cur[:, None] - 1)
    score = jnp.where(forced[:, None, :], FORCED_SCORE, jnp.where(causal_blk[:, None, :], p_blk, -jnp.inf))
    _, idx = lax.top_k(score, min(N_SELECT, nb))
    valid = jnp.take_along_axis(jnp.broadcast_to(causal_blk[:, None, :], score.shape), idx, axis=-1)
    g_ix = jnp.arange(NSA_KV)[None, :, None]
    kg = ksb[g_ix, idx]
    vg = vsb[g_ix, idx]
    n_sel = idx.shape[-1] * SEL_BLOCK
    spos = idx[..., None] * SEL_BLOCK + jnp.arange(SEL_BLOCK)
    smask = (valid[..., None] & (spos <= qpos[:, None, None, None])).reshape(tq, NSA_KV, 1, n_sel)
    s_s = jnp.einsum('tghd,tgsnd->tghsn', q, kg).reshape(tq, NSA_KV, NSA_HPG, n_sel)
    p_s = _masked_softmax(s_s, smask)
    o_s = jnp.einsum('tghm,tgmd->tghd', p_s, vg.reshape(tq, NSA_KV, n_sel, HEAD_DIM))
    dist = qpos[:, None] - wpos[None, :]
    wmask = (wpos[None, :] >= 0) & (dist >= 0) & (dist < WINDOW)
    s_w = jnp.einsum('tghd,ngd->tghn', q, kw)
    p_w = _masked_softmax(s_w, wmask[:, None, None, :])
    o_w = jnp.einsum('tghn,ngd->tghd', p_w, vw)
    return gate[..., 0:1] * o_c + gate[..., 1:2] * o_s + gate[..., 2:3] * o_w


def _mixer_sample(x, fkv_past, logf_past, nkv_past, win_buf, g_mix, w_all, fb, cmp_pe, cmp_w1, cmp_w2):
    bd, s, d = x.shape
    past = fkv_past.shape[1]
    wb = win_buf.shape[1]
    n = bd * s
    pos = past + jnp.arange(s)
    cos, sin = _rope_tables(pos)
    cos = jnp.tile(cos, (bd, 1))
    sin = jnp.tile(sin, (bd, 1))
    fq, fkv, _, misc, nq, nkv, _ = _project(x.reshape(n, d), g_mix, w_all, fb, cos, sin, _tile(n, 256))
    logf = misc[:, :FOX_HEADS].reshape(bd, s, FOX_HEADS)
    gate = misc[:, FOX_HEADS:FOX_HEADS + N_GATES].reshape(bd, s, NSA_KV, NSA_HPG, 3)
    fkv_new = fkv.reshape(bd, s, 2, FOX_HEADS, HEAD_DIM)
    fkv_all = jnp.concatenate([fkv_past, fkv_new], axis=1)
    c_all = jnp.cumsum(jnp.concatenate([logf_past.astype(F32), logf], axis=1), axis=1)
    q = fq.astype(F32).reshape(bd, s, FOX_HEADS, HEAD_DIM)
    kpos = jnp.arange(past + s)
    sc = jnp.einsum('bqhd,bkhd->bhqk', q, fkv_all[:, :, 0]).astype(F32)
    sc = sc + jnp.swapaxes(c_all[:, past:], 1, 2)[..., :, None] - jnp.swapaxes(c_all, 1, 2)[..., None, :]
    p = _masked_softmax(sc, kpos[None, :] <= pos[:, None])
    of = jnp.einsum('bhqk,bkhd->bqhd', p, fkv_all[:, :, 1]).reshape(bd, s, FOX_W)
    inv = np.argsort(_pair_perm(1))
    qn = nq.astype(F32).reshape(bd, s, NSA_HEADS, HEAD_DIM)[:, :, inv].reshape(bd, s, NSA_KV, NSA_HPG, HEAD_DIM)
    rows = nkv.reshape(bd, s, 6, NSA_KV, HEAD_DIM)
    nkv_all = jnp.concatenate([nkv_past, rows[:, :, :4]], axis=1)
    kvc = _compress_rows(jnp.stack([nkv_all[:, :, 0], nkv_all[:, :, 1]]), cmp_pe, cmp_w1, cmp_w2)
    win_all = jnp.concatenate([win_buf, rows[:, :, 4:]], axis=1)
    wpos = jnp.concatenate([past - wb + jnp.arange(wb), pos])
    f = jax.vmap(_nsa_block_sample, in_axes=(0, None, 0, 0, 0, 0, 0, 0, 0, None))
    on = f(qn, pos, gate, kvc[0], kvc[1], nkv_all[:, :, 2], nkv_all[:, :, 3], win_all[:, :, 0], win_all[:, :, 1], wpos)
    on = on.reshape(bd, s, NSA_HEADS, HEAD_DIM)[:, :, np.asarray(_NSA_PAIR_ORDER)].reshape(bd, s, NSA_W)
    return of, on, fkv_new, logf, rows[:, :, :4], win_all[:, s:]


def _rmsnorm_kernel(x_ref, g_ref, o_ref):
    x = x_ref[...]
    ms = jnp.mean(x * x, axis=-1, keepdims=True)
    o_ref[...] = (x * lax.rsqrt(ms + RMS_EPS)) * g_ref[...]


def _rmsnorm(x2, g, tm):
    n, d = x2.shape
    return pl.pallas_call(
        _rmsnorm_kernel,
        grid=(n // tm,),
        in_specs=[pl.BlockSpec((tm, d), lambda i: (i, 0)), pl.BlockSpec((1, d), lambda i: (0, 0))],
        out_specs=pl.BlockSpec((tm, d), lambda i: (i, 0)),
        out_shape=jax.ShapeDtypeStruct((n, d), F32),
        compiler_params=_cparams("parallel"),
    )(x2, g.reshape(1, d))


def kernel(x_prompt, x_sample, cache_fox_kv, cache_fox_logf, cache_nsa_kv, state_win_kv, page_table,
           norm_mix, w_in, fox_fb, cmp_pe, cmp_w1, cmp_w2, norm_fox, norm_nsa, w_out,
           norm_ffn, peer_wq, peer_subkeys, peer_u, peer_v, norm_final):
    depth = w_in.shape[0]
    b, t, d = x_prompt.shape
    bd, s, _ = x_sample.shape
    n_seq, n_pages = page_table.shape
    past = n_pages * cache_fox_kv.shape[2]
    np_, ns = b * t, bd * s
    xp = x_prompt.reshape(np_, d)
    xs = x_sample.reshape(ns, d)
    perm = _pair_perm(HEAD_DIM)
    outs = [[] for _ in range(8)]
    for l in range(depth):
        w_all, fb = _prep_w_in(w_in[l], fox_fb[l])
        mp = _mixer_prompt(xp.reshape(b, t, d), norm_mix[l], w_all, fb, cmp_pe[l], cmp_w1[l], cmp_w2[l])
        fkv_past = cache_fox_kv[l, page_table].reshape((n_seq, past) + cache_fox_kv.shape[3:])
        lf_past = cache_fox_logf[l, page_table].reshape(n_seq, past, FOX_HEADS)
        nkv_past = cache_nsa_kv[l, page_table].reshape((n_seq, past) + cache_nsa_kv.shape[3:])
        msm = _mixer_sample(xs.reshape(bd, s, d), fkv_past, lf_past, nkv_past, state_win_kv[l],
                            norm_mix[l], w_all, fb, cmp_pe[l], cmp_w1[l], cmp_w2[l])
        w_f = w_out[l, :FOX_W].astype(BF16)
        w_n = w_out[l, FOX_W:][perm].astype(BF16)
        g_n = norm_nsa[l][perm]
        xp = _merge(xp, mp[0].reshape(np_, FOX_W), mp[1].reshape(np_, NSA_W), norm_fox[l], g_n, w_f, w_n,
                    _tile(np_, 512))
        xs = _merge(xs, msm[0].reshape(ns, FOX_W), msm[1].reshape(ns, NSA_W), norm_fox[l], g_n, w_f, w_n,
                    _tile(ns, 256))
        wqt = peer_wq[l].T.astype(BF16)
        sk = peer_subkeys[l].astype(BF16)
        u_bf = peer_u[l].astype(BF16)
        vt_bf = peer_v[l].T.astype(BF16)
        xp = _peer_channel_t(xp.T, norm_ffn[l], wqt, sk, u_bf, vt_bf, _tile(np_, 512)).T
        xs = _peer_channel_t(xs.T, norm_ffn[l], wqt, sk, u_bf, vt_bf, _tile(ns, 256)).T
        for i in range(4):
            outs[i].append(mp[2 + i])
            outs[4 + i].append(msm[2 + i])
    y_prompt = _rmsnorm(xp, norm_final, _tile(np_, 512)).reshape(b, t, d)
    y_sample = _rmsnorm(xs, norm_final, _tile(ns, 256)).reshape(bd, s, d)
    return (y_prompt, y_sample) + tuple(jnp.stack(o) for o in outs)
```

```python
import functools

import jax
import jax.numpy as jnp
import numpy as np
from jax import lax
from jax.experimental import pallas as pl
from jax.experimental.pallas import tpu as pltpu

HEAD_DIM = 64
FOX_HEADS = 8
NSA_HEADS = 8
NSA_KV = 2
NSA_HPG = NSA_HEADS // NSA_KV
FOX_W = FOX_HEADS * HEAD_DIM
NSA_W = NSA_HEADS * HEAD_DIM
NSA_KV_W = NSA_KV * HEAD_DIM
CMP_LEN = 32
CMP_STRIDE = 16
SEL_BLOCK = 64
N_SELECT = 16
WINDOW = 512
FORCED_SCORE = 1e4
ROPE_THETA = 10000.0
ATTN_SCALE = HEAD_DIM ** -0.5
PEER_KEYS = 128
PEER_HEADS = 8
PEER_QDIM = 256
PEER_TOPK = 16
RMS_EPS = 1e-6

LANES = 128
VMEM_LIMIT = 56 * 1024 * 1024
BF16 = jnp.bfloat16
F32 = jnp.float32
NEG_INF = float("-inf")


def _cparams(*sem):
    return pltpu.CompilerParams(dimension_semantics=sem, vmem_limit_bytes=VMEM_LIMIT)


def _gelu(x):
    return 0.5 * x * (1.0 + lax.erf(x * np.float32(np.sqrt(0.5))))


def _extract_topk(x, k):
    n_rows = x.shape[0]
    iota = lax.broadcasted_iota(jnp.int32, x.shape, 0)
    rank = jnp.full(x.shape, k, jnp.int32)
    vals, idxs = [], []
    for r in range(k):
        m = jnp.max(x, axis=0, keepdims=True)
        i = jnp.min(jnp.where(x == m, iota, n_rows), axis=0, keepdims=True)
        hit = iota == i
        vals.append(m)
        idxs.append(i)
        rank = jnp.where(hit, r, rank)
        x = jnp.where(hit, NEG_INF, x)
    return vals, idxs, rank


_PEER_CAND = [(a, b) for a in range(PEER_TOPK) for b in range(PEER_TOPK) if (a + 1) * (b + 1) <= PEER_TOPK]


def _peer_route_kernel(xt_ref, g_ref, wqt_ref, sk_ref, ht_ref, r1_ref, cnt_ref, a_ref, b_ref):
    x = xt_ref[...]
    ms = jnp.mean(x * x, axis=0, keepdims=True)
    h = (x * lax.rsqrt(ms + RMS_EPS)) * g_ref[...]
    hb = h.astype(BF16)
    ht_ref[...] = hb
    q = jnp.dot(wqt_ref[...], hb, preferred_element_type=F32)
    half = PEER_QDIM // 2
    tn = x.shape[1]
    lane_w = min(LANES, tn)
    for hd, cb in [(hd, cb) for hd in range(PEER_HEADS) for cb in range(tn // lane_w)]:
        cols = slice(cb * lane_w, (cb + 1) * lane_w)
        q0 = q[hd * PEER_QDIM: hd * PEER_QDIM + half, cols].astype(BF16)
        q1 = q[hd * PEER_QDIM + half: (hd + 1) * PEER_QDIM, cols].astype(BF16)
        s0 = jnp.dot(sk_ref[0], q0, preferred_element_type=F32)
        s1 = jnp.dot(sk_ref[1], q1, preferred_element_type=F32)
        v0, i0, _ = _extract_topk(s0, PEER_TOPK)
        v1, _, rank1 = _extract_topk(s1, PEER_TOPK)
        cand = jnp.concatenate([v0[a] + v1[b] for a, b in _PEER_CAND], axis=0)
        fv, _, crank = _extract_topk(cand, PEER_TOPK)
        sel = crank < PEER_TOPK
        z = jnp.zeros_like(fv[0])
        for r in range(PEER_TOPK):
            z = z + jnp.exp(fv[r] - fv[0])
        inv_z = 1.0 / z
        row = 0
        iota = lax.broadcasted_iota(jnp.int32, s0.shape, 0)
        cnt_full = jnp.zeros(s0.shape, F32)
        for a in range(PEER_TOPK):
            nb = PEER_TOPK // (a + 1)
            cnt_a = jnp.sum(sel[row:row + nb].astype(F32), axis=0, keepdims=True)
            row += nb
            cnt_full = jnp.where(iota == i0[a], cnt_a, cnt_full)
        sl = slice(hd * PEER_KEYS, (hd + 1) * PEER_KEYS)
        r1_ref[sl, cols] = rank1.astype(F32).astype(BF16)
        cnt_ref[sl, cols] = cnt_full
        a_ref[sl, cols] = jnp.exp(s0 - v0[0])
        b_ref[sl, cols] = (jnp.exp(s1 - v1[0]) * inv_z).astype(BF16)


def _peer_dense_kernel(xt_ref, ht_ref, r1_ref, cnt_ref, a_ref, b_ref, u0_ref, un_ref, vt_ref, o_ref, pre_ref,
                       *, rows_per_step):
    c = pl.program_id(1)
    pair = 2 * PEER_KEYS
    n_pairs = rows_per_step // 2

    @pl.when(c == 0)
    def _():
        o_ref[...] = xt_ref[...]
        pre_ref[...] = jnp.dot(u0_ref[...], ht_ref[...], preferred_element_type=F32)

    acc = None
    for jp in range(n_pairs):
        parts = []
        for j in range(2 * jp, 2 * jp + 2):
            act = _gelu(pre_ref[j * PEER_KEYS:(j + 1) * PEER_KEYS, :])
            i1 = c * rows_per_step + j
            g = jnp.zeros(act.shape, BF16)
            for hd in range(PEER_HEADS):
                sl = slice(hd * PEER_KEYS, (hd + 1) * PEER_KEYS)
                cnt_row = cnt_ref[pl.ds(hd * PEER_KEYS + i1, 1), :].astype(BF16)
                a_row = a_ref[pl.ds(hd * PEER_KEYS + i1, 1), :].astype(BF16)
                g = g + jnp.where(r1_ref[sl, :] < cnt_row, b_ref[sl, :] * a_row, jnp.zeros_like(g))
            parts.append(g * act.astype(BF16))
        rows = slice(jp * pair, (jp + 1) * pair)
        pre_ref[rows, :] = jnp.dot(un_ref[rows, :], ht_ref[...], preferred_element_type=F32)
        part = jnp.dot(vt_ref[:, rows], jnp.concatenate(parts, axis=0), preferred_element_type=F32)
        acc = part if acc is None else acc + part
    o_ref[...] += acc


def _peer_channel_t(xt, g_ffn, wqt, sk, u_bf, vt_bf, tn, rows_per_step=8):
    d, n = xt.shape
    nt = n // tn
    hk = PEER_HEADS * PEER_KEYS
    tok = lambda i: (0, i)
    full = lambda i: (0, 0)
    ht, r1, cnt, a, b = pl.pallas_call(
        _peer_route_kernel,
        grid=(nt,),
        in_specs=[pl.BlockSpec((d, tn), tok), pl.BlockSpec((d, 1), full),
                  pl.BlockSpec(wqt.shape, full), pl.BlockSpec(sk.shape, lambda i: (0, 0, 0))],
        out_specs=[pl.BlockSpec((d, tn), tok)] + [pl.BlockSpec((hk, tn), tok)] * 4,
        out_shape=[jax.ShapeDtypeStruct((d, n), BF16)] + [jax.ShapeDtypeStruct((hk, n), dt) for dt in (BF16, F32, F32, BF16)],
        compiler_params=_cparams("parallel"),
    )(xt, g_ffn.reshape(d, 1), wqt, sk)
    ec = rows_per_step * PEER_KEYS
    n_chunks = u_bf.shape[0] // ec
    tok2 = lambda i, c: (0, i)
    return pl.pallas_call(
        functools.partial(_peer_dense_kernel, rows_per_step=rows_per_step),
        grid=(nt, n_chunks),
        in_specs=[pl.BlockSpec((d, tn), tok2), pl.BlockSpec((d, tn), tok2)]
                 + [pl.BlockSpec((hk, tn), tok2)] * 4
                 + [pl.BlockSpec((ec, d), lambda i, c: (0, 0)),
                    pl.BlockSpec((ec, d), lambda i, c: (jnp.minimum(c + 1, n_chunks - 1), 0)),
                    pl.BlockSpec((d, ec), lambda i, c: (0, c))],
        out_specs=pl.BlockSpec((d, tn), tok2),
        out_shape=jax.ShapeDtypeStruct((d, n), F32),
        scratch_shapes=[pltpu.VMEM((ec, tn), F32)],
        compiler_params=_cparams("parallel", "arbitrary"),
    )(xt, ht, r1, cnt, a, b, u_bf, u_bf, vt_bf)


_C_FQ = 0
_C_FKV = _C_FQ + FOX_W
_C_MISC = _C_FKV + 2 * FOX_W
_C_NQ = _C_MISC + LANES
_C_NQR = _C_NQ + NSA_W
_C_NKV = _C_NQR + NSA_W
_C_NKR = _C_NKV + 6 * NSA_KV_W
_C_END = _C_NKR + 3 * NSA_KV_W
N_GATES = 3 * NSA_HEADS


def _rot_cols(w):
    d, c = w.shape
    half = HEAD_DIM // 2
    w4 = w.reshape(d, c // HEAD_DIM, 2, half)
    return jnp.stack([-w4[:, :, 1], w4[:, :, 0]], axis=2).reshape(d, c)


_NSA_PAIR_ORDER = [g * NSA_HPG + j for j in range(NSA_HPG) for g in range(NSA_KV)]


def _pair_perm(n_per_head):
    return np.concatenate([np.arange(h * n_per_head, (h + 1) * n_per_head) for h in _NSA_PAIR_ORDER])


def _prep_w_in(w_in, fox_fb):
    d = w_in.shape[0]
    o = 0
    fq = w_in[:, o:o + FOX_W]; o += FOX_W
    fkv = w_in[:, o:o + 2 * FOX_W]; o += 2 * FOX_W
    ff = w_in[:, o:o + FOX_HEADS]; o += FOX_HEADS
    nq = w_in[:, o:o + NSA_W]; o += NSA_W
    nkv = w_in[:, o:o + 6 * NSA_KV_W]; o += 6 * NSA_KV_W
    ng = w_in[:, o:o + N_GATES]
    misc = jnp.concatenate([ff, ng, jnp.zeros((d, LANES - FOX_HEADS - N_GATES), w_in.dtype)], axis=1)
    nq = nq[:, _pair_perm(HEAD_DIM)]
    nk = nkv.reshape(d, 3, 2, NSA_KV_W)[:, :, 0].reshape(d, 3 * NSA_KV_W)
    w_all = jnp.concatenate([fq, fkv, misc, nq, _rot_cols(nq), nkv, _rot_cols(nk)], axis=1).astype(BF16)
    fb = jnp.concatenate([fox_fb.astype(F32), jnp.zeros((LANES - FOX_HEADS,), F32)]).reshape(1, LANES)
    return w_all, fb


def _rope_tables(pos):
    half = HEAD_DIM // 2
    inv = ROPE_THETA ** (-jnp.arange(half, dtype=F32) / half)
    ang = pos.astype(F32)[:, None] * inv[None, :]
    reps = LANES // half
    return jnp.tile(jnp.cos(ang), (1, reps)), jnp.tile(jnp.sin(ang), (1, reps))


def _proj_kernel(x_ref, g_ref, w_ref, fb_ref, cos_ref, sin_ref,
                 fq_ref, fkv_ref, fkvb_ref, misc_ref, nq_ref, nkv_ref, nkvb_ref):
    x = x_ref[...]
    ms = jnp.mean(x * x, axis=-1, keepdims=True)
    hb = ((x * lax.rsqrt(ms + RMS_EPS)) * g_ref[...]).astype(BF16)

    def mm(c0, c1):
        return jnp.dot(hb, w_ref[:, c0:c1], preferred_element_type=F32)

    fq_ref[...] = (mm(_C_FQ, _C_FKV) * ATTN_SCALE).astype(BF16)
    fkv = mm(_C_FKV, _C_MISC)
    fkv_ref[...] = fkv
    fkvb_ref[...] = fkv.astype(BF16)
    z = mm(_C_MISC, _C_NQ) + fb_ref[...]
    lane = lax.broadcasted_iota(jnp.int32, z.shape, 1)
    log_sig = jnp.minimum(z, 0.0) - jnp.log1p(jnp.exp(-jnp.abs(z)))
    misc_ref[...] = jnp.where(lane < FOX_HEADS, log_sig, jax.nn.sigmoid(z))
    cos = cos_ref[...]
    sin = sin_ref[...]
    reps = NSA_W // LANES
    cos_q = jnp.concatenate([cos] * reps, axis=1)
    sin_q = jnp.concatenate([sin] * reps, axis=1)
    nq = mm(_C_NQ, _C_NQR) * cos_q + mm(_C_NQR, _C_NKV) * sin_q
    nq_ref[...] = (nq * ATTN_SCALE).astype(BF16)
    for s in range(6):
        c0 = _C_NKV + s * NSA_KV_W
        v = mm(c0, c0 + NSA_KV_W)
        if s % 2 == 0:
            r0 = _C_NKR + (s // 2) * NSA_KV_W
            v = v * cos + mm(r0, r0 + NSA_KV_W) * sin
        nkv_ref[:, s * NSA_KV_W:(s + 1) * NSA_KV_W] = v
        nkvb_ref[:, s * NSA_KV_W:(s + 1) * NSA_KV_W] = v.astype(BF16)


def _project(x2, g_mix, w_all, fb, cos, sin, tm):
    n, d = x2.shape
    tok = lambda i: (i, 0)
    full = lambda i: (0, 0)
    widths = [(FOX_W, BF16), (2 * FOX_W, F32), (2 * FOX_W, BF16), (LANES, F32), (NSA_W, BF16),
              (6 * NSA_KV_W, F32), (6 * NSA_KV_W, BF16)]
    return pl.pallas_call(
        _proj_kernel,
        grid=(n // tm,),
        in_specs=[pl.BlockSpec((tm, d), tok), pl.BlockSpec((1, d), full), pl.BlockSpec(w_all.shape, full),
                  pl.BlockSpec((1, LANES), full), pl.BlockSpec((tm, LANES), tok), pl.BlockSpec((tm, LANES), tok)],
        out_specs=[pl.BlockSpec((tm, w), tok) for w, _ in widths],
        out_shape=[jax.ShapeDtypeStruct((n, w), dt) for w, dt in widths],
        compiler_params=_cparams("parallel"),
    )(x2, g_mix.reshape(1, d), w_all, fb, cos, sin)


def _merge_kernel(x_ref, of_ref, on_ref, gf_ref, gn_ref, wf_ref, wn_ref, o_ref):
    def norm(v, g):
        v = v.astype(F32)
        ms = jnp.mean(v * v, axis=-1, keepdims=True)
        return ((v * lax.rsqrt(ms + RMS_EPS)) * g).astype(BF16)

    yf = norm(of_ref[...], gf_ref[...])
    yn = norm(on_ref[...], gn_ref[...])
    o_ref[...] = (x_ref[...] + jnp.dot(yf, wf_ref[...], preferred_element_type=F32)
                  + jnp.dot(yn, wn_ref[...], preferred_element_type=F32))


def _merge(x2, of, on, g_fox, g_nsa, w_f, w_n, tm):
    n, d = x2.shape
    tok = lambda i: (i, 0)
    full = lambda i: (0, 0)
    return pl.pallas_call(
        _merge_kernel,
        grid=(n // tm,),
        in_specs=[pl.BlockSpec((tm, d), tok), pl.BlockSpec((tm, FOX_W), tok), pl.BlockSpec((tm, NSA_W), tok),
                  pl.BlockSpec((1, FOX_W), full), pl.BlockSpec((1, NSA_W), full),
                  pl.BlockSpec((FOX_W, d), full), pl.BlockSpec((NSA_W, d), full)],
        out_specs=pl.BlockSpec((tm, d), tok),
        out_shape=jax.ShapeDtypeStruct((n, d), F32),
        compiler_params=_cparams("parallel"),
    )(x2, of, on, g_fox.reshape(1, -1), g_nsa.reshape(1, -1), w_f, w_n)


_NT = (((1,), (1,)), ((), ()))


def _flash_init(m_ref, l_ref, acc_ref):
    m_ref[...] = jnp.full(m_ref.shape, NEG_INF, F32)
    l_ref[...] = jnp.zeros(l_ref.shape, F32)
    acc_ref[...] = jnp.zeros(acc_ref.shape, F32)


def _flash_steps(qhs, load_k, load_v, bias_fn, lo, hi, tk, m_ref, l_ref, acc_ref):
    def body(kt, carry):
        ks = pl.multiple_of(kt * tk, tk)
        k = load_k(ks)
        v = load_v(ks)
        biases = bias_fn(ks)
        for n, qh in enumerate(qhs):
            s = lax.dot_general(qh, k, _NT, preferred_element_type=F32) + biases[n]
            m_prev = m_ref[n]
            m_new = jnp.maximum(m_prev, jnp.max(s, axis=1, keepdims=True))
            m_safe = jnp.where(m_new == NEG_INF, 0.0, m_new)
            p = jnp.exp(s - _lane_repeat(m_safe, tk))
            alpha = jnp.exp(m_prev - m_safe)
            l_ref[n] = alpha * l_ref[n] + jnp.sum(p, axis=1, keepdims=True)
            acc_ref[n] = alpha * acc_ref[n] + jnp.dot(p.astype(BF16), v, preferred_element_type=F32)
            m_ref[n] = m_new
        return carry

    lax.fori_loop(lo, hi, body, 0)


def _lane_repeat(x, width):
    return jnp.concatenate([x] * (width // LANES), axis=1)


def _flash_finish(l_ref, acc_ref):
    return acc_ref[...] / jnp.maximum(l_ref[...], 1e-30)


def _half_mask(shape, hh):
    lane = lax.broadcasted_iota(jnp.int32, shape, 1)
    return (lane >= HEAD_DIM) if hh else (lane < HEAD_DIM)


def _causal_bias(bias, qstart, ks):
    qpos = qstart + lax.broadcasted_iota(jnp.int32, bias.shape, 0)
    kpos = ks + lax.broadcasted_iota(jnp.int32, bias.shape, 1)
    return jnp.where(kpos <= qpos, bias, NEG_INF)


def _fox_prompt_kernel(q_ref, k_ref, v_ref, cq_ref, ck_ref, o_ref, m_ref, l_ref, acc_ref, *, tq):
    j = pl.program_id(1)
    qi = pl.program_id(2)
    qstart = qi * tq
    qp = q_ref[0]
    load_k = lambda ks: k_ref[0, pl.ds(ks, tq), :]
    load_v = lambda ks: v_ref[0, pl.ds(ks, tq), :]
    halves = [_half_mask(qp.shape, hh) for hh in range(2)]
    qhs = [jnp.where(half, qp, jnp.zeros_like(qp)) for half in halves]
    cqs = [_lane_repeat(jnp.broadcast_to(cq_ref[0, 0, :, hh:hh + 1], (tq, LANES)), tq) for hh in range(2)]
    bias = lambda ks: [cqs[hh] - ck_ref[0, pl.ds(2 * j + hh, 1), pl.ds(ks, tq)] for hh in range(2)]
    bias_diag = lambda ks: [_causal_bias(bb, qstart, ks) for bb in bias(ks)]
    _flash_init(m_ref, l_ref, acc_ref)
    _flash_steps(qhs, load_k, load_v, bias, 0, qi, tq, m_ref, l_ref, acc_ref)
    _flash_steps(qhs, load_k, load_v, bias_diag, qi, qi + 1, tq, m_ref, l_ref, acc_ref)
    o = _flash_finish(l_ref, acc_ref)
    o_ref[0] = jnp.where(halves[0], o[0], o[1])


def _fox_prompt(fq, fkvb, c, tq):
    b, t, _ = fq.shape
    n_pairs = FOX_W // LANES
    cq = c.reshape(b, t, n_pairs, 2).transpose(0, 2, 1, 3)
    ck = c.transpose(0, 2, 1)
    return pl.pallas_call(
        functools.partial(_fox_prompt_kernel, tq=tq),
        grid=(b, n_pairs, t // tq),
        in_specs=[pl.BlockSpec((1, tq, LANES), lambda bi, j, qi: (bi, qi, j)),
                  pl.BlockSpec((1, t, LANES), lambda bi, j, qi: (bi, 0, j)),
                  pl.BlockSpec((1, t, LANES), lambda bi, j, qi: (bi, 0, n_pairs + j)),
                  pl.BlockSpec((1, 1, tq, 2), lambda bi, j, qi: (bi, j, qi, 0)),
                  pl.BlockSpec((1, FOX_HEADS, t), lambda bi, j, qi: (bi, 0, 0))],
        out_specs=pl.BlockSpec((1, tq, LANES), lambda bi, j, qi: (bi, qi, j)),
        out_shape=jax.ShapeDtypeStruct((b, t, FOX_W), F32),
        scratch_shapes=[pltpu.VMEM((2, tq, LANES), F32), pltpu.VMEM((2, tq, LANES), F32), pltpu.VMEM((2, tq, LANES), F32)],
        compiler_params=_cparams("parallel", "parallel", "arbitrary"),
    )(fq, fkvb, fkvb, cq, ck)


def _compress_kernel(x_ref, pe_ref, w1_ref, w2_ref, o_ref):
    x = (x_ref[0] + pe_ref[0]).astype(BF16)
    h = _gelu(jnp.dot(x, w1_ref[0], preferred_element_type=F32))
    o_ref[0] = jnp.dot(h.astype(BF16), w2_ref[0], preferred_element_type=F32)


def _compress(xw, pe, w1, w2, tr):
    _, r, kd = xw.shape
    return pl.pallas_call(
        _compress_kernel,
        grid=(2, r // tr),
        in_specs=[pl.BlockSpec((1, tr, kd), lambda s, i: (s, i, 0)),
                  pl.BlockSpec((1, 1, kd), lambda s, i: (s, 0, 0)),
                  pl.BlockSpec((1, kd, HEAD_DIM), lambda s, i: (s, 0, 0)),
                  pl.BlockSpec((1, HEAD_DIM, HEAD_DIM), lambda s, i: (s, 0, 0))],
        out_specs=pl.BlockSpec((1, tr, HEAD_DIM), lambda s, i: (s, i, 0)),
        out_shape=jax.ShapeDtypeStruct((2, r, HEAD_DIM), F32),
        compiler_params=_cparams("parallel", "parallel"),
    )(xw, pe.reshape(2, 1, kd), w1.astype(BF16), w2.astype(BF16))


def _cmp_windows(rows):
    two, b, length, g, d = rows.shape
    nch = length // CMP_STRIDE
    ch = rows[:, :, :nch * CMP_STRIDE].reshape(two, b, nch, CMP_STRIDE, g, d).transpose(0, 1, 2, 4, 3, 5)
    ch = ch.reshape(two, b, nch, g, CMP_STRIDE * d)
    reps = CMP_LEN // CMP_STRIDE
    nc = nch - reps + 1
    return jnp.concatenate([ch[:, :, r:r + nc] for r in range(reps)], axis=-1)


def _round_up(x, m):
    return -(-x // m) * m


def _compress_rows(rows, pe, w1, w2):
    xw = _cmp_windows(rows)
    two, b, nc, g, kd = xw.shape
    r = b * nc * g
    tr = min(512, _round_up(r, 8))
    rp = _round_up(r, tr)
    xw = jnp.pad(xw.reshape(2, r, kd), ((0, 0), (0, rp - r), (0, 0)))
    return _compress(xw, pe, w1, w2, tr)[:, :r].reshape(2, b, nc, g, HEAD_DIM)


def _sel_cover_t(nc, nb, ncp):
    cs = np.arange(ncp)[None, :] * CMP_STRIDE
    bs = np.arange(nb)[:, None] * SEL_BLOCK
    cov = np.clip(np.minimum(cs + CMP_LEN, bs + SEL_BLOCK) - np.maximum(cs, bs), 0, None)
    cov = np.where(np.arange(ncp)[None, :] < nc, cov, 0)
    return jnp.asarray(cov.astype(np.float32) / CMP_LEN, BF16)


def _nsa_prompt_kernel(q_ref, kv_ref, kc_ref, vc_ref, misc_ref, covt_ref, o_ref,
                       qh_ref, oc_ref, bias_ref, m_ref, l_ref, acc_ref, *, tq, nc, nb, n_sel):
    qi = pl.program_id(1)
    qstart = qi * tq
    ncp = kc_ref.shape[1]
    n_pairs = NSA_W // LANES
    load = lambda slot: (lambda ks: kv_ref[0, pl.ds(ks, tq), slot * LANES:(slot + 1) * LANES])

    cidx = lax.broadcasted_iota(jnp.int32, (tq, ncp), 1)
    qpos_c = qstart + lax.broadcasted_iota(jnp.int32, (tq, ncp), 0)
    cmask = (cidx * CMP_STRIDE + (CMP_LEN - 1) <= qpos_c) & (cidx < nc)
    psum = [jnp.zeros((tq, ncp), F32) for _ in range(NSA_KV)]
    for j in range(n_pairs):
        qp = q_ref[0, :, j * LANES:(j + 1) * LANES]
        for g in range(NSA_KV):
            h = j * NSA_KV + g
            qh = jnp.where(_half_mask(qp.shape, g), qp, jnp.zeros_like(qp))
            qh_ref[h] = qh
            s = lax.dot_general(qh, kc_ref[0], _NT, preferred_element_type=F32)
            s = jnp.where(cmask, s, NEG_INF)
            m = jnp.max(s, axis=1, keepdims=True)
            m = jnp.where(m == NEG_INF, 0.0, m)
            e = jnp.exp(s - m)
            p = e / jnp.maximum(jnp.sum(e, axis=1, keepdims=True), 1e-30)
            psum[g] = psum[g] + p
            oc_ref[h] = jnp.dot(p.astype(BF16), vc_ref[0], preferred_element_type=F32)

    for g in range(NSA_KV):
        hi = psum[g].astype(BF16)
        lo = (psum[g] - hi.astype(F32)).astype(BF16)
        pb = (lax.dot_general(covt_ref[...], hi, _NT, preferred_element_type=F32)
              + lax.dot_general(covt_ref[...], lo, _NT, preferred_element_type=F32))
        jj = lax.broadcasted_iota(jnp.int32, (nb, tq), 0)
        cur = (qstart + lax.broadcasted_iota(jnp.int32, (nb, tq), 1)) // SEL_BLOCK
        causal = jj <= cur
        forced = (jj == 0) | (jj == cur) | (jj == cur - 1)
        score = jnp.where(forced, FORCED_SCORE, jnp.where(causal, pb, NEG_INF))
        _, _, rank = _extract_topk(score, n_sel)
        sel_t = jnp.where((rank < n_sel) & causal, 1.0, 0.0)
        sel_t = jnp.concatenate([sel_t, jnp.zeros((LANES - nb, tq), F32)], axis=0)
        sel = sel_t.T.astype(BF16)

        def fill(kt, carry):
            ks = pl.multiple_of(kt * tq, tq)
            blk = (ks + lax.broadcasted_iota(jnp.int32, (LANES, tq), 1)) // SEL_BLOCK
            expand = jnp.where(lax.broadcasted_iota(jnp.int32, (LANES, tq), 0) == blk, 1.0, 0.0).astype(BF16)
            hit = jnp.dot(sel, expand, preferred_element_type=F32)
            bias = jnp.where(hit > 0.5, 0.0, NEG_INF)
            bias_ref[g, :, pl.ds(ks, tq)] = _causal_bias(bias, qstart, ks)
            return carry

        lax.fori_loop(0, qi + 1, fill, 0)

    def win_bias(ks):
        qpos = qstart + lax.broadcasted_iota(jnp.int32, (tq, tq), 0)
        kpos = ks + lax.broadcasted_iota(jnp.int32, (tq, tq), 1)
        dist = qpos - kpos
        return jnp.where((dist >= 0) & (dist < WINDOW), 0.0, NEG_INF)

    win_lo = jnp.maximum(qi - (-(-WINDOW // tq)), 0)
    sel_bias = lambda ks: [bias_ref[g, :, pl.ds(ks, tq)] for g in range(NSA_KV)]
    win_bias2 = lambda ks: [win_bias(ks)] * NSA_KV
    for j in range(n_pairs):
        qhs = [qh_ref[j * NSA_KV + g] for g in range(NSA_KV)]
        _flash_init(m_ref, l_ref, acc_ref)
        _flash_steps(qhs, load(2), load(3), sel_bias, 0, qi + 1, tq, m_ref, l_ref, acc_ref)
        o_s = _flash_finish(l_ref, acc_ref)
        _flash_init(m_ref, l_ref, acc_ref)
        _flash_steps(qhs, load(4), load(5), win_bias2, win_lo, qi + 1, tq, m_ref, l_ref, acc_ref)
        o_w = _flash_finish(l_ref, acc_ref)
        outs = []
        for g in range(NSA_KV):
            col = FOX_HEADS + (g * NSA_HPG + j) * 3
            gates = [misc_ref[0, :, col + r:col + r + 1] for r in range(3)]
            outs.append(gates[0] * oc_ref[j * NSA_KV + g] + gates[1] * o_s[g] + gates[2] * o_w[g])
        o_ref[0, :, j * LANES:(j + 1) * LANES] = jnp.where(_half_mask(outs[0].shape, 0), outs[0], outs[1])


def _nsa_prompt(nq, nkvb, kc, vc, misc, tq):
    b, t, _ = nq.shape
    nc = kc.shape[1]
    ncp = _round_up(nc, LANES)
    nb = -(-t // SEL_BLOCK)
    assert nb <= LANES and t % tq == 0 and tq % SEL_BLOCK == 0
    pad = ((0, 0), (0, ncp - nc), (0, 0))
    kcp = jnp.pad(kc, pad).astype(BF16)
    vcp = jnp.pad(vc, pad).astype(BF16)
    covt = _sel_cover_t(nc, nb, ncp)
    kern = functools.partial(_nsa_prompt_kernel, tq=tq, nc=nc, nb=nb, n_sel=min(N_SELECT, nb))
    seq = lambda bi, qi: (bi, 0, 0)
    tile = lambda bi, qi: (bi, qi, 0)
    return pl.pallas_call(
        kern,
        grid=(b, t // tq),
        in_specs=[pl.BlockSpec((1, tq, NSA_W), tile), pl.BlockSpec((1, t, 6 * NSA_KV_W), seq),
                  pl.BlockSpec((1, ncp, LANES), seq), pl.BlockSpec((1, ncp, LANES), seq),
                  pl.BlockSpec((1, tq, LANES), tile), pl.BlockSpec((nb, ncp), lambda bi, qi: (0, 0))],
        out_specs=pl.BlockSpec((1, tq, NSA_W), tile),
        out_shape=jax.ShapeDtypeStruct((b, t, NSA_W), F32),
        scratch_shapes=[pltpu.VMEM((NSA_HEADS, tq, LANES), BF16), pltpu.VMEM((NSA_HEADS, tq, LANES), F32),
                        pltpu.VMEM((NSA_KV, tq, t), F32),
                        pltpu.VMEM((NSA_KV, tq, LANES), F32), pltpu.VMEM((NSA_KV, tq, LANES), F32),
                        pltpu.VMEM((NSA_KV, tq, LANES), F32)],
        compiler_params=_cparams("parallel", "arbitrary"),
    )(nq, nkvb, kcp, vcp, misc, covt)


def _tile(n, pref):
    t = min(pref, n)
    while n % t:
        t //= 2
    return t


def _mixer_prompt(x, g_mix, w_all, fb, cmp_pe, cmp_w1, cmp_w2):
    b, t, d = x.shape
    n = b * t
    cos, sin = _rope_tables(jnp.arange(t))
    cos = jnp.tile(cos, (b, 1))
    sin = jnp.tile(sin, (b, 1))
    fq, fkv, fkvb, misc, nq, nkv, nkvb = _project(x.reshape(n, d), g_mix, w_all, fb, cos, sin, _tile(n, 512))
    logf = misc[:, :FOX_HEADS].reshape(b, t, FOX_HEADS)
    c = jnp.cumsum(logf, axis=1)
    tq = _tile(t, 512)
    of = _fox_prompt(fq.reshape(b, t, FOX_W), fkvb.reshape(b, t, 2 * FOX_W), c, tq)
    rows = nkv.reshape(b, t, 6, NSA_KV, HEAD_DIM)
    kvc = _compress_rows(jnp.stack([rows[:, :, 0], rows[:, :, 1]]), cmp_pe, cmp_w1, cmp_w2)
    nc = kvc.shape[2]
    on = _nsa_prompt(nq.reshape(b, t, NSA_W), nkvb.reshape(b, t, 6 * NSA_KV_W),
                     kvc[0].reshape(b, nc, NSA_KV_W), kvc[1].reshape(b, nc, NSA_KV_W),
                     misc.reshape(b, t, LANES), tq)
    wb = min(WINDOW, t)
    return (of, on, fkv.reshape(b, t, 2, FOX_HEADS, HEAD_DIM), logf, rows[:, :, :4], rows[:, t - wb:, 4:])


def _split3(x):
    hi = x.astype(BF16)
    r = x - hi.astype(F32)
    mid = r.astype(BF16)
    return hi, mid, (r - mid.astype(F32)).astype(BF16)


def _split2(x):
    hi = x.astype(BF16)
    return hi, (x - hi.astype(F32)).astype(BF16)


def _softmax_update(s_list, v_list, m_ref, l_ref, acc_ref):
    m_prev = m_ref[...]
    m_new = m_prev
    for s in s_list:
        m_new = jnp.maximum(m_new, jnp.max(s, axis=1, keepdims=True))
    m_safe = jnp.where(m_new == NEG_INF, 0.0, m_new)
    alpha = jnp.exp(m_prev - m_safe)
    l_new = alpha * l_ref[...]
    acc = alpha * acc_ref[...]
    for s, v_bf in zip(s_list, v_list):
        p = jnp.exp(s - m_safe)
        l_new = l_new + jnp.sum(p, axis=1, keepdims=True)
        acc = acc + jnp.dot(p.astype(BF16), v_bf, preferred_element_type=F32)
    l_ref[...] = l_new
    acc_ref[...] = acc
    m_ref[...] = m_new


def _fox_sample_kernel(pt_ref, q_ref, cnew_ref, knew_ref, vnew_ref, bnew_ref, scan_ref, *rest, n_pg, s_tok):
    kv_refs, lf_refs = rest[:n_pg], rest[n_pg:2 * n_pg]
    o_ref, m_ref, l_ref, acc_ref, carry_ref = rest[2 * n_pg:]
    st = pl.program_id(1)

    @pl.when(st == 0)
    def _():
        _flash_init(m_ref, l_ref, acc_ref)
        carry_ref[...] = jnp.zeros(carry_ref.shape, F32)

    q = q_ref[0]
    cnew = cnew_ref[0]
    page_rows = kv_refs[0].shape[0]
    carry = carry_ref[...]
    s_list, v_list = [], []
    for i in reversed(range(n_pg)):
        x = lf_refs[i][...]
        sc = sum(jnp.dot(p, scan_ref[...], preferred_element_type=F32) for p in _split3(x))
        d = sc[:, :page_rows] + carry
        carry = carry + sc[:, page_rows:]
        dd = jnp.concatenate([jnp.broadcast_to(d[h:h + 1], (s_tok, page_rows)) for h in range(FOX_HEADS)], axis=0)
        page = kv_refs[i][...]
        s_list.append(lax.dot_general(q, page[:, :FOX_W].astype(BF16), _NT, preferred_element_type=F32) + (cnew + dd))
        v_list.append(page[:, FOX_W:].astype(BF16))
    carry_ref[...] = carry
    _softmax_update(s_list, v_list, m_ref, l_ref, acc_ref)

    @pl.when(st == pl.num_programs(1) - 1)
    def _():
        s = lax.dot_general(q, knew_ref[0], _NT, preferred_element_type=F32) + bnew_ref[0]
        _softmax_update([s], [vnew_ref[0]], m_ref, l_ref, acc_ref)
        o = _flash_finish(l_ref, acc_ref)
        lane_head = lax.broadcasted_iota(jnp.int32, (s_tok, FOX_W), 1) // HEAD_DIM
        out = jnp.zeros((s_tok, FOX_W), F32)
        for h in range(FOX_HEADS):
            out = jnp.where(lane_head == h, o[h * s_tok:(h + 1) * s_tok], out)
        o_ref[0] = out


def _fox_sample(layer, page_table, q_rows, cnew_b, knew, vnew, bnew, cache_kv, logf_t, n_pg):
    bd, n_pages = page_table.shape
    page_rows = cache_kv.shape[2]
    s_tok = q_rows.shape[1] // FOX_HEADS
    rows = q_rows.shape[1]
    kidx = np.arange(page_rows)
    scan = np.concatenate([(kidx[:, None] > kidx[None, :]), np.ones((page_rows, page_rows), bool)], axis=1)
    scan = jnp.asarray(scan, BF16)
    n_st = n_pages // n_pg
    seq = lambda b, st, pt: (b, 0, 0)

    def page_map(i):
        return lambda b, st, pt: (layer, pt[b, n_pages - (st + 1) * n_pg + i], 0, 0)

    grid_spec = pltpu.PrefetchScalarGridSpec(
        num_scalar_prefetch=1,
        grid=(bd, n_st),
        in_specs=[pl.BlockSpec((1, rows, FOX_W), seq), pl.BlockSpec((1, rows, page_rows), seq),
                  pl.BlockSpec((1, page_rows, FOX_W), seq), pl.BlockSpec((1, page_rows, FOX_W), seq),
                  pl.BlockSpec((1, rows, page_rows), seq),
                  pl.BlockSpec(scan.shape, lambda b, st, pt: (0, 0))]
                 + [pl.BlockSpec((None, None, page_rows, 2 * FOX_W), page_map(i)) for i in range(n_pg)]
                 + [pl.BlockSpec((None, None, FOX_HEADS, page_rows), page_map(i)) for i in range(n_pg)],
        out_specs=pl.BlockSpec((1, s_tok, FOX_W), seq),
        scratch_shapes=[pltpu.VMEM((rows, 1), F32), pltpu.VMEM((rows, 1), F32), pltpu.VMEM((rows, FOX_W), F32),
                        pltpu.VMEM((FOX_HEADS, page_rows), F32)])
    return pl.pallas_call(
        functools.partial(_fox_sample_kernel, n_pg=n_pg, s_tok=s_tok),
        grid_spec=grid_spec,
        out_shape=jax.ShapeDtypeStruct((bd, s_tok, FOX_W), F32),
        compiler_params=_cparams("parallel", "arbitrary"),
    )(page_table, q_rows, cnew_b, knew, vnew, bnew, scan, *([cache_kv] * n_pg), *([logf_t] * n_pg))


def _nsa_cmp_sample_kernel(pt_ref, pe_ref, w1_ref, w2_ref, *rest, n_pg):
    pages = rest[:n_pg]
    kc_ref, vc_ref, carry_ref, xs_ref = rest[n_pg:]
    st = pl.program_id(1)

    @pl.when(st == 0)
    def _():
        carry_ref[...] = jnp.zeros(carry_ref.shape, F32)

    page_rows = pages[0].shape[0]
    m = n_pg * page_rows // CMP_STRIDE
    last = lax.broadcasted_iota(jnp.int32, (m, NSA_KV_W), 0) == m - 1
    for slot, out_ref in ((0, kc_ref), (1, vc_ref)):
        for i, pg in enumerate(pages):
            xs_ref[i * page_rows:(i + 1) * page_rows, :] = pg[:, slot * NSA_KV_W:(slot + 1) * NSA_KV_W]
        a = jnp.zeros((m, NSA_KV_W), F32)
        b = jnp.zeros((m, NSA_KV_W), F32)
        for r in range(CMP_STRIDE):
            x = xs_ref[pl.ds(r, m, stride=CMP_STRIDE), :]
            xa = (x + pe_ref[slot, r:r + 1, :]).astype(BF16)
            xb = (x + pe_ref[slot, CMP_STRIDE + r:CMP_STRIDE + r + 1, :]).astype(BF16)
            a = a + jnp.dot(xa, w1_ref[slot, r], preferred_element_type=F32)
            b = b + jnp.dot(xb, w1_ref[slot, CMP_STRIDE + r], preferred_element_type=F32)
        b_next = jnp.where(last, carry_ref[slot, 0:1, :], pltpu.roll(b, m - 1, axis=0))
        carry_ref[slot] = b
        h = _gelu(a + b_next).astype(BF16)
        out_ref[0] = jnp.dot(h, w2_ref[slot], preferred_element_type=F32).astype(BF16)


def _nsa_cmp_sample(layer, page_table, cache_nsa, pe2, w1_bd, w2_bd, n_pg):
    bd, n_pages = page_table.shape
    page_rows = cache_nsa.shape[2]
    cpp = page_rows // CMP_STRIDE
    m = n_pg * cpp
    n_st = n_pages // n_pg
    const = lambda nd: (lambda b, st, pt: (0,) * nd)

    def page_map(i):
        return lambda b, st, pt: (layer, pt[b, n_pages - (st + 1) * n_pg + i], 0, 0)

    out_spec = pl.BlockSpec((1, m, NSA_KV_W), lambda b, st, pt: (b, n_st - 1 - st, 0))
    grid_spec = pltpu.PrefetchScalarGridSpec(
        num_scalar_prefetch=1,
        grid=(bd, n_st),
        in_specs=[pl.BlockSpec(pe2.shape, const(3)), pl.BlockSpec(w1_bd.shape, const(4)),
                  pl.BlockSpec(w2_bd.shape, const(3))]
                 + [pl.BlockSpec((None, None, page_rows, cache_nsa.shape[3]), page_map(i)) for i in range(n_pg)],
        out_specs=[out_spec, out_spec],
        scratch_shapes=[pltpu.VMEM((2, m, NSA_KV_W), F32), pltpu.VMEM((n_pg * page_rows, NSA_KV_W), F32)])
    shape = jax.ShapeDtypeStruct((bd, n_pages * cpp, NSA_KV_W), BF16)
    return pl.pallas_call(
        functools.partial(_nsa_cmp_sample_kernel, n_pg=n_pg),
        grid_spec=grid_spec,
        out_shape=[shape, shape],
        compiler_params=_cparams("parallel", "arbitrary"),
    )(page_table, pe2, w1_bd, w2_bd, *([cache_nsa] * n_pg))


def _nsa_sample_kernel(pt_ref, q_ref, kc_ref, vc_ref, gate_ref, covt_ref, gsum_ref, win_ref, new_ref, *rest,
                       n_pg, s_tok, nc, nb, n_sel, past):
    pages = rest[:n_pg]
    o_ref, bias_ref, oc_ref, m_ref, l_ref, acc_ref = rest[n_pg:]
    st = pl.program_id(1)
    q = q_ref[0]
    rows = q.shape[0]
    page_rows = pages[0].shape[0]
    sel_k = slice(2 * NSA_KV_W, 3 * NSA_KV_W)
    sel_v = slice(3 * NSA_KV_W, 4 * NSA_KV_W)

    @pl.when(st == 0)
    def _():
        ncp = kc_ref.shape[1]
        nbp = covt_ref.shape[0]
        s = lax.dot_general(q, kc_ref[0], _NT, preferred_element_type=F32)
        cidx = lax.broadcasted_iota(jnp.int32, (rows, ncp), 1)
        qpos = past + lax.broadcasted_iota(jnp.int32, (rows, ncp), 0) % s_tok
        s = jnp.where((cidx * CMP_STRIDE + (CMP_LEN - 1) <= qpos) & (cidx < nc), s, NEG_INF)
        m = jnp.max(s, axis=1, keepdims=True)
        m = jnp.where(m == NEG_INF, 0.0, m)
        e = jnp.exp(s - m)
        p = e / jnp.maximum(jnp.sum(e, axis=1, keepdims=True), 1e-30)
        oc_ref[...] = jnp.dot(p.astype(BF16), vc_ref[0], preferred_element_type=F32)
        psum = sum(jnp.dot(gsum_ref[...], x, preferred_element_type=F32) for x in _split2(p))
        pb = sum(lax.dot_general(covt_ref[...], x, _NT, preferred_element_type=F32) for x in _split2(psum))
        jj = lax.broadcasted_iota(jnp.int32, (nbp, rows), 0)
        cur = (past + lax.broadcasted_iota(jnp.int32, (nbp, rows), 1) % s_tok) // SEL_BLOCK
        causal = (jj <= cur) & (jj < nb)
        forced = (jj == 0) | (jj == cur) | (jj == cur - 1)
        score = jnp.where(forced, FORCED_SCORE, jnp.where(causal, pb, NEG_INF))
        _, _, rank = _extract_topk(score, n_sel)
        sel = jnp.where((rank < n_sel) & causal, 1.0, 0.0).T.astype(BF16)
        chunk = min(past, 16 * page_rows)

        def fill(c, carry):
            ks = pl.multiple_of(c * chunk, chunk)
            blk = (ks + lax.broadcasted_iota(jnp.int32, (nbp, chunk), 1)) // SEL_BLOCK
            expand = jnp.where(lax.broadcasted_iota(jnp.int32, (nbp, chunk), 0) == blk, 1.0, 0.0).astype(BF16)
            hit = jnp.dot(sel, expand, preferred_element_type=F32)
            bias_ref[:, pl.ds(ks, chunk)] = jnp.where(hit > 0.5, 0.0, NEG_INF)
            return carry

        lax.fori_loop(0, bias_ref.shape[1] // chunk, fill, 0)
        _flash_init(m_ref, l_ref, acc_ref)

    s_list, v_list = [], []
    for i in range(n_pg):
        ks = pl.multiple_of((st * n_pg + i) * page_rows, page_rows)
        page = pages[i][...]
        s = lax.dot_general(q, page[:, sel_k].astype(BF16), _NT, preferred_element_type=F32)
        s_list.append(s + bias_ref[:, pl.ds(ks, page_rows)])
        v_list.append(page[:, sel_v].astype(BF16))
    _softmax_update(s_list, v_list, m_ref, l_ref, acc_ref)

    @pl.when(st == pl.num_programs(1) - 1)
    def _():
        tok = lax.broadcasted_iota(jnp.int32, (rows, page_rows), 0) % s_tok
        col = lax.broadcasted_iota(jnp.int32, (rows, page_rows), 1)
        bnew = jnp.where((col <= tok) & (col < s_tok), 0.0, NEG_INF)
        s = lax.dot_general(q, new_ref[0, 0], _NT, preferred_element_type=F32) + bnew
        _softmax_update([s], [new_ref[0, 1]], m_ref, l_ref, acc_ref)
        o_s = _flash_finish(l_ref, acc_ref)
        wb = win_ref.shape[0]
        win = win_ref[...]
        wtok = lax.broadcasted_iota(jnp.int32, (rows, wb), 0) % s_tok
        wi = lax.broadcasted_iota(jnp.int32, (rows, wb), 1)
        dist = wb + wtok - wi
        ok = (past - wb + wi >= 0) & (dist >= 0) & (dist < WINDOW)
        s1 = lax.dot_general(q, win[:, :NSA_KV_W].astype(BF16), _NT, preferred_element_type=F32)
        s1 = jnp.where(ok, s1, NEG_INF)
        s2 = lax.dot_general(q, new_ref[0, 2], _NT, preferred_element_type=F32) + bnew
        m = jnp.maximum(jnp.max(s1, axis=1, keepdims=True), jnp.max(s2, axis=1, keepdims=True))
        e1 = jnp.exp(s1 - m)
        e2 = jnp.exp(s2 - m)
        den = jnp.sum(e1, axis=1, keepdims=True) + jnp.sum(e2, axis=1, keepdims=True)
        o_w = (jnp.dot(e1.astype(BF16), win[:, NSA_KV_W:].astype(BF16), preferred_element_type=F32)
               + jnp.dot(e2.astype(BF16), new_ref[0, 3], preferred_element_type=F32)) / den
        o = gate_ref[0, 0] * oc_ref[...] + gate_ref[0, 1] * o_s + gate_ref[0, 2] * o_w
        for j in range(NSA_W // LANES):
            lo = o[(2 * j) * s_tok:(2 * j + 1) * s_tok]
            hi = o[(2 * j + 1) * s_tok:(2 * j + 2) * s_tok]
            o_ref[0, :, j * LANES:(j + 1) * LANES] = jnp.where(_half_mask(lo.shape, 0), lo, hi)


def _nsa_sample(layer, page_table, q_rows, kc, vc, gates, win_state, new_kv, cache_nsa, s_tok, n_pg):
    bd, n_pages = page_table.shape
    page_rows = cache_nsa.shape[2]
    past = n_pages * page_rows
    rows = q_rows.shape[1]
    ncp = kc.shape[1]
    nc = ncp - 1
    nb = -(-(past + s_tok) // SEL_BLOCK)
    nbp = _round_up(nb, LANES)
    assert past % SEL_BLOCK == 0 and s_tok <= SEL_BLOCK and s_tok < CMP_STRIDE
    covt = _sel_cover_t(nc, nbp, ncp)
    r = np.arange(rows)
    same = (r[:, None] % s_tok == r[None, :] % s_tok) & ((r[:, None] // s_tok) % NSA_KV == (r[None, :] // s_tok) % NSA_KV)
    live = r < NSA_HEADS * s_tok
    gsum = jnp.asarray(same & live[:, None] & live[None, :], BF16)
    wb = win_state.shape[2]
    n_st = n_pages // n_pg
    seq = lambda b, st, pt: (b, 0, 0)
    const2 = lambda b, st, pt: (0, 0)
    grid_spec = pltpu.PrefetchScalarGridSpec(
        num_scalar_prefetch=1,
        grid=(bd, n_st),
        in_specs=[pl.BlockSpec((1, rows, LANES), seq), pl.BlockSpec((1, ncp, LANES), seq),
                  pl.BlockSpec((1, ncp, LANES), seq), pl.BlockSpec((1, 3, rows, LANES), lambda b, st, pt: (b, 0, 0, 0)),
                  pl.BlockSpec(covt.shape, const2), pl.BlockSpec(gsum.shape, const2),
                  pl.BlockSpec((None, None, wb, 2 * NSA_KV_W), lambda b, st, pt: (layer, b, 0, 0)),
                  pl.BlockSpec((1, 4, page_rows, LANES), lambda b, st, pt: (b, 0, 0, 0))]
                 + [pl.BlockSpec((None, None, page_rows, cache_nsa.shape[3]),
                                 (lambda i: lambda b, st, pt: (layer, pt[b, st * n_pg + i], 0, 0))(i))
                    for i in range(n_pg)],
        out_specs=pl.BlockSpec((1, s_tok, NSA_W), seq),
        scratch_shapes=[pltpu.VMEM((rows, past), F32), pltpu.VMEM((rows, LANES), F32),
                        pltpu.VMEM((rows, 1), F32), pltpu.VMEM((rows, 1), F32), pltpu.VMEM((rows, LANES), F32)])
    kern = functools.partial(_nsa_sample_kernel, n_pg=n_pg, s_tok=s_tok, nc=nc, nb=nb, n_sel=min(N_SELECT, nb),
                             past=past)
    return pl.pallas_call(
        kern,
        grid_spec=grid_spec,
        out_shape=jax.ShapeDtypeStruct((bd, s_tok, NSA_W), F32),
        compiler_params=_cparams("parallel", "arbitrary"),
    )(page_table, q_rows, kc, vc, gates, covt, gsum, win_state, new_kv, *([cache_nsa] * n_pg))


def _block_diag_groups(w):
    eye = jnp.eye(NSA_KV, dtype=w.dtype)
    out = jnp.einsum('gh,...de->...gdhe', eye, w)
    return out.reshape(w.shape[:-2] + (NSA_KV_W, NSA_KV_W))


def _pad_rows(x, rows, value=0.0):
    pad = [(0, 0)] * x.ndim
    pad[-2] = (0, rows - x.shape[-2])
    return jnp.pad(x, pad, constant_values=value)


def _mixer_sample(x, layer, page_table, cache_fox_kv, logf_t, cache_nsa_kv, state_win, g_mix, w_all, fb,
                  cmp_pe, cmp_w1, cmp_w2):
    bd, s, d = x.shape
    n_pages = page_table.shape[1]
    page_rows = cache_fox_kv.shape[2]
    past = n_pages * page_rows
    n = bd * s
    pos = past + jnp.arange(s)
    cos, sin = _rope_tables(pos)
    cos = jnp.tile(cos, (bd, 1))
    sin = jnp.tile(sin, (bd, 1))
    fq, fkv, fkvb, misc, nq, nkv, nkvb = _project(x.reshape(n, d), g_mix, w_all, fb, cos, sin, _tile(n, 256))
    logf = misc[:, :FOX_HEADS].reshape(bd, s, FOX_HEADS)
    cnew = jnp.cumsum(logf, axis=1).transpose(0, 2, 1)
    rows_f = FOX_HEADS * s
    cnew_b = jnp.broadcast_to(cnew.reshape(bd, rows_f, 1), (bd, rows_f, page_rows))
    causal = jnp.arange(s)[None, :] <= jnp.arange(s)[:, None]
    bnew = jnp.where(causal, cnew[:, :, :, None] - cnew[:, :, None, :], NEG_INF).reshape(bd, rows_f, s)
    bnew = jnp.pad(bnew, ((0, 0), (0, 0), (0, page_rows - s)), constant_values=NEG_INF)
    head_of_lane = jnp.arange(FOX_W) // HEAD_DIM
    fq3 = fq.reshape(bd, 1, s, FOX_W)
    q_rows = jnp.where(head_of_lane[None, None, None, :] == jnp.arange(FOX_HEADS)[None, :, None, None], fq3,
                       jnp.zeros_like(fq3)).reshape(bd, rows_f, FOX_W)
    fkvb3 = fkvb.reshape(bd, s, 2 * FOX_W)
    knew = _pad_rows(fkvb3[:, :, :FOX_W], page_rows)
    vnew = _pad_rows(fkvb3[:, :, FOX_W:], page_rows)
    of = _fox_sample(layer, page_table, q_rows, cnew_b, knew, vnew, bnew, cache_fox_kv, logf_t, _tile(n_pages, 8))
    pe2 = jnp.tile(cmp_pe, (1, 1, NSA_KV))
    w1_bd = _block_diag_groups(cmp_w1.reshape(2, CMP_LEN, HEAD_DIM, HEAD_DIM)).astype(BF16)
    w2_bd = _block_diag_groups(cmp_w2).astype(BF16)
    n_pg = _tile(n_pages, 16)
    kc, vc = _nsa_cmp_sample(layer, page_table, cache_nsa_kv, pe2, w1_bd, w2_bd, n_pg)
    rows_n = _round_up(NSA_HEADS * s, LANES)
    nq4 = nq.reshape(bd, s, NSA_W // LANES, 1, LANES).transpose(0, 2, 3, 1, 4)
    half_of_lane = (jnp.arange(LANES) // HEAD_DIM)[None, None, None, None, :]
    qn = jnp.where(half_of_lane == jnp.arange(NSA_KV)[None, None, :, None, None], nq4, jnp.zeros_like(nq4))
    qn = _pad_rows(qn.reshape(bd, NSA_HEADS * s, LANES), rows_n)
    gate = misc[:, FOX_HEADS:FOX_HEADS + N_GATES].reshape(bd, s, NSA_KV, NSA_HPG, 3)
    gate = gate.transpose(0, 4, 3, 2, 1).reshape(bd, 3, NSA_HEADS * s, 1)
    gates = jnp.broadcast_to(_pad_rows(gate, rows_n), (bd, 3, rows_n, LANES))
    nkvb3 = nkvb.reshape(bd, s, 6, NSA_KV_W)
    new_kv = _pad_rows(nkvb3[:, :, 2:6].transpose(0, 2, 1, 3), page_rows)
    on = _nsa_sample(layer, page_table, qn, kc, vc, gates, state_win, new_kv, cache_nsa_kv, s, n_pg)
    rows = nkv.reshape(bd, s, 6, NSA_KV, HEAD_DIM)
    wb = state_win.shape[2]
    win_all = jnp.concatenate([state_win[layer].reshape(bd, wb, 2, NSA_KV, HEAD_DIM), rows[:, :, 4:]], axis=1)
    return of, on, fkv.reshape(bd, s, 2, FOX_HEADS, HEAD_DIM), logf, rows[:, :, :4], win_all[:, s:]


def _rmsnorm_kernel(x_ref, g_ref, o_ref):
    x = x_ref[...]
    ms = jnp.mean(x * x, axis=-1, keepdims=True)
    o_ref[...] = (x * lax.rsqrt(ms + RMS_EPS)) * g_ref[...]


def _rmsnorm(x2, g, tm):
    n, d = x2.shape
    return pl.pallas_call(
        _rmsnorm_kernel,
        grid=(n // tm,),
        in_specs=[pl.BlockSpec((tm, d), lambda i: (i, 0)), pl.BlockSpec((1, d), lambda i: (0, 0))],
        out_specs=pl.BlockSpec((tm, d), lambda i: (i, 0)),
        out_shape=jax.ShapeDtypeStruct((n, d), F32),
        compiler_params=_cparams("parallel"),
    )(x2, g.reshape(1, d))


def kernel(x_prompt, x_sample, cache_fox_kv, cache_fox_logf, cache_nsa_kv, state_win_kv, page_table,
           norm_mix, w_in, fox_fb, cmp_pe, cmp_w1, cmp_w2, norm_fox, norm_nsa, w_out,
           norm_ffn, peer_wq, peer_subkeys, peer_u, peer_v, norm_final):
    depth = w_in.shape[0]
    b, t, d = x_prompt.shape
    bd, s, _ = x_sample.shape
    n_seq, n_pages = page_table.shape
    past = n_pages * cache_fox_kv.shape[2]
    np_, ns = b * t, bd * s
    xp = x_prompt.reshape(np_, d)
    xs = x_sample.reshape(ns, d)
    perm = _pair_perm(HEAD_DIM)
    outs = [[] for _ in range(8)]
    pool, page_rows = cache_fox_kv.shape[1:3]
    fox_kv2 = cache_fox_kv.reshape(depth, pool, page_rows, 2 * FOX_W)
    logf_t = cache_fox_logf.transpose(0, 1, 3, 2)
    nsa_kv2 = cache_nsa_kv.reshape(depth, pool, page_rows, 4 * NSA_KV_W)
    win2 = state_win_kv.reshape(depth, bd, state_win_kv.shape[2], 2 * NSA_KV_W)
    for l in range(depth):
        w_all, fb = _prep_w_in(w_in[l], fox_fb[l])
        mp = _mixer_prompt(xp.reshape(b, t, d), norm_mix[l], w_all, fb, cmp_pe[l], cmp_w1[l], cmp_w2[l])
        msm = _mixer_sample(xs.reshape(bd, s, d), l, page_table, fox_kv2, logf_t, nsa_kv2, win2,
                            norm_mix[l], w_all, fb, cmp_pe[l], cmp_w1[l], cmp_w2[l])
        w_f = w_out[l, :FOX_W].astype(BF16)
        w_n = w_out[l, FOX_W:][perm].astype(BF16)
        g_n = norm_nsa[l][perm]
        xp = _merge(xp, mp[0].reshape(np_, FOX_W), mp[1].reshape(np_, NSA_W), norm_fox[l], g_n, w_f, w_n,
                    _tile(np_, 512))
        xs = _merge(xs, msm[0].reshape(ns, FOX_W), msm[1].reshape(ns, NSA_W), norm_fox[l], g_n, w_f, w_n,
                    _tile(ns, 256))
        wqt = peer_wq[l].T.astype(BF16)
        sk = peer_subkeys[l].astype(BF16)
        u_bf = peer_u[l].astype(BF16)
        vt_bf = peer_v[l].T.astype(BF16)
        xp = _peer_channel_t(xp.T, norm_ffn[l], wqt, sk, u_bf, vt_bf, _tile(np_, 512)).T
        xs = _peer_channel_t(xs.T, norm_ffn[l], wqt, sk, u_bf, vt_bf, _tile(ns, 256)).T
        for i in range(4):
            outs[i].append(mp[2 + i])
            outs[4 + i].append(msm[2 + i])
    y_prompt = _rmsnorm(xp, norm_final, _tile(np_, 512)).reshape(b, t, d)
    y_sample = _rmsnorm(xs, norm_final, _tile(ns, 256)).reshape(bd, s, d)
    return (y_prompt, y_sample) + tuple(jnp.stack(o) for o in outs)
```

```python
import functools

import jax
import jax.numpy as jnp
import numpy as np
from jax import lax
from jax.experimental import pallas as pl
from jax.experimental.pallas import tpu as pltpu

HEAD_DIM = 64
FOX_HEADS = 8
NSA_HEADS = 8
NSA_KV = 2
NSA_HPG = NSA_HEADS // NSA_KV
FOX_W = FOX_HEADS * HEAD_DIM
NSA_W = NSA_HEADS * HEAD_DIM
NSA_KV_W = NSA_KV * HEAD_DIM
CMP_LEN = 32
CMP_STRIDE = 16
SEL_BLOCK = 64
N_SELECT = 16
WINDOW = 512
FORCED_SCORE = 1e4
ROPE_THETA = 10000.0
ATTN_SCALE = HEAD_DIM ** -0.5
PEER_KEYS = 128
PEER_HEADS = 8
PEER_QDIM = 256
PEER_TOPK = 16
RMS_EPS = 1e-6

LANES = 128
VMEM_LIMIT = 56 * 1024 * 1024
BF16 = jnp.bfloat16
F32 = jnp.float32
NEG_INF = float("-inf")


def _cparams(*sem):
    return pltpu.CompilerParams(dimension_semantics=sem, vmem_limit_bytes=VMEM_LIMIT)


def _gelu(x):
    return 0.5 * x * (1.0 + lax.erf(x * np.float32(np.sqrt(0.5))))


def _extract_topk(x, k):
    n_rows = x.shape[0]
    iota = lax.broadcasted_iota(jnp.int32, x.shape, 0)
    rank = jnp.full(x.shape, k, jnp.int32)
    vals, idxs = [], []
    for r in range(k):
        m = jnp.max(x, axis=0, keepdims=True)
        i = jnp.min(jnp.where(x == m, iota, n_rows), axis=0, keepdims=True)
        hit = iota == i
        vals.append(m)
        idxs.append(i)
        rank = jnp.where(hit, r, rank)
        x = jnp.where(hit, NEG_INF, x)
    return vals, idxs, rank


_PEER_CAND = [(a, b) for a in range(PEER_TOPK) for b in range(PEER_TOPK) if (a + 1) * (b + 1) <= PEER_TOPK]


def _peer_route_kernel(xt_ref, g_ref, wqt_ref, sk_ref, ht_ref, r1_ref, cnt_ref, a_ref, b_ref):
    x = xt_ref[...]
    ms = jnp.mean(x * x, axis=0, keepdims=True)
    h = (x * lax.rsqrt(ms + RMS_EPS)) * g_ref[...]
    hb = h.astype(BF16)
    ht_ref[...] = hb
    q = jnp.dot(wqt_ref[...], hb, preferred_element_type=F32)
    half = PEER_QDIM // 2
    tn = x.shape[1]
    lane_w = min(LANES, tn)
    for hd, cb in [(hd, cb) for hd in range(PEER_HEADS) for cb in range(tn // lane_w)]:
        cols = slice(cb * lane_w, (cb + 1) * lane_w)
        q0 = q[hd * PEER_QDIM: hd * PEER_QDIM + half, cols].astype(BF16)
        q1 = q[hd * PEER_QDIM + half: (hd + 1) * PEER_QDIM, cols].astype(BF16)
        s0 = jnp.dot(sk_ref[0], q0, preferred_element_type=F32)
        s1 = jnp.dot(sk_ref[1], q1, preferred_element_type=F32)
        v0, i0, _ = _extract_topk(s0, PEER_TOPK)
        v1, _, rank1 = _extract_topk(s1, PEER_TOPK)
        cand = jnp.concatenate([v0[a] + v1[b] for a, b in _PEER_CAND], axis=0)
        fv, _, crank = _extract_topk(cand, PEER_TOPK)
        sel = crank < PEER_TOPK
        z = jnp.zeros_like(fv[0])
        for r in range(PEER_TOPK):
            z = z + jnp.exp(fv[r] - fv[0])
        inv_z = 1.0 / z
        row = 0
        iota = lax.broadcasted_iota(jnp.int32, s0.shape, 0)
        cnt_full = jnp.zeros(s0.shape, F32)
        for a in range(PEER_TOPK):
            nb = PEER_TOPK // (a + 1)
            cnt_a = jnp.sum(sel[row:row + nb].astype(F32), axis=0, keepdims=True)
            row += nb
            cnt_full = jnp.where(iota == i0[a], cnt_a, cnt_full)
        sl = slice(hd * PEER_KEYS, (hd + 1) * PEER_KEYS)
        r1_ref[sl, cols] = rank1.astype(F32).astype(BF16)
        cnt_ref[sl, cols] = cnt_full
        a_ref[sl, cols] = jnp.exp(s0 - v0[0])
        b_ref[sl, cols] = (jnp.exp(s1 - v1[0]) * inv_z).astype(BF16)


def _peer_dense_kernel(xt_ref, ht_ref, r1_ref, cnt_ref, a_ref, b_ref, u0_ref, un_ref, vt_ref, o_ref, *pre_refs,
                       rows_per_step):
    c = pl.program_id(1)
    pair = 2 * PEER_KEYS
    n_pairs = rows_per_step // 2

    @pl.when(c == 0)
    def _():
        o_ref[...] = xt_ref[...]
        for jp in range(n_pairs):
            pre_refs[jp][...] = jnp.dot(u0_ref[jp * pair:(jp + 1) * pair, :], ht_ref[...],
                                        preferred_element_type=F32)

    acc = None
    for jp in range(n_pairs):
        parts = []
        for j in range(2 * jp, 2 * jp + 2):
            act = _gelu(pre_refs[jp][(j - 2 * jp) * PEER_KEYS:(j - 2 * jp + 1) * PEER_KEYS, :])
            i1 = c * rows_per_step + j
            g = jnp.zeros(act.shape, BF16)
            for hd in range(PEER_HEADS):
                sl = slice(hd * PEER_KEYS, (hd + 1) * PEER_KEYS)
                cnt_row = cnt_ref[pl.ds(hd * PEER_KEYS + i1, 1), :].astype(BF16)
                a_row = a_ref[pl.ds(hd * PEER_KEYS + i1, 1), :].astype(BF16)
                g = g + jnp.where(r1_ref[sl, :] < cnt_row, b_ref[sl, :] * a_row, jnp.zeros_like(g))
            parts.append(g * act.astype(BF16))
        rows = slice(jp * pair, (jp + 1) * pair)
        pre_refs[jp][...] = jnp.dot(un_ref[rows, :], ht_ref[...], preferred_element_type=F32)
        part = jnp.dot(vt_ref[:, rows], jnp.concatenate(parts, axis=0), preferred_element_type=F32)
        acc = part if acc is None else acc + part
    o_ref[...] += acc


def _peer_channel_t(xt, g_ffn, wqt, sk, u_bf, vt_bf, tn, rows_per_step=8):
    d, n = xt.shape
    nt = n // tn
    hk = PEER_HEADS * PEER_KEYS
    tok = lambda i: (0, i)
    full = lambda i: (0, 0)
    ht, r1, cnt, a, b = pl.pallas_call(
        _peer_route_kernel,
        grid=(nt,),
        in_specs=[pl.BlockSpec((d, tn), tok), pl.BlockSpec((d, 1), full),
                  pl.BlockSpec(wqt.shape, full), pl.BlockSpec(sk.shape, lambda i: (0, 0, 0))],
        out_specs=[pl.BlockSpec((d, tn), tok)] + [pl.BlockSpec((hk, tn), tok)] * 4,
        out_shape=[jax.ShapeDtypeStruct((d, n), BF16)] + [jax.ShapeDtypeStruct((hk, n), dt) for dt in (BF16, F32, F32, BF16)],
        compiler_params=_cparams("parallel"),
    )(xt, g_ffn.reshape(d, 1), wqt, sk)
    ec = rows_per_step * PEER_KEYS
    n_chunks = u_bf.shape[0] // ec
    tok2 = lambda i, c: (0, i)
    return pl.pallas_call(
        functools.partial(_peer_dense_kernel, rows_per_step=rows_per_step),
        grid=(nt, n_chunks),
        in_specs=[pl.BlockSpec((d, tn), tok2), pl.BlockSpec((d, tn), tok2)]
                 + [pl.BlockSpec((hk, tn), tok2)] * 4
                 + [pl.BlockSpec((ec, d), lambda i, c: (0, 0)),
                    pl.BlockSpec((ec, d), lambda i, c: (jnp.minimum(c + 1, n_chunks - 1), 0)),
                    pl.BlockSpec((d, ec), lambda i, c: (0, c))],
        out_specs=pl.BlockSpec((d, tn), tok2),
        out_shape=jax.ShapeDtypeStruct((d, n), F32),
        scratch_shapes=[pltpu.VMEM((2 * PEER_KEYS, tn), F32)] * (rows_per_step // 2),
        compiler_params=_cparams("parallel", "arbitrary"),
    )(xt, ht, r1, cnt, a, b, u_bf, u_bf, vt_bf)


_C_FQ = 0
_C_FKV = _C_FQ + FOX_W
_C_MISC = _C_FKV + 2 * FOX_W
_C_NQ = _C_MISC + LANES
_C_NQR = _C_NQ + NSA_W
_C_NKV = _C_NQR + NSA_W
_C_NKR = _C_NKV + 6 * NSA_KV_W
_C_END = _C_NKR + 3 * NSA_KV_W
N_GATES = 3 * NSA_HEADS


def _rot_cols(w):
    d, c = w.shape
    half = HEAD_DIM // 2
    w4 = w.reshape(d, c // HEAD_DIM, 2, half)
    return jnp.stack([-w4[:, :, 1], w4[:, :, 0]], axis=2).reshape(d, c)


_NSA_PAIR_ORDER = [g * NSA_HPG + j for j in range(NSA_HPG) for g in range(NSA_KV)]


def _pair_perm(n_per_head):
    return np.concatenate([np.arange(h * n_per_head, (h + 1) * n_per_head) for h in _NSA_PAIR_ORDER])


def _prep_w_in(w_in, fox_fb):
    d = w_in.shape[0]
    o = 0
    fq = w_in[:, o:o + FOX_W]; o += FOX_W
    fkv = w_in[:, o:o + 2 * FOX_W]; o += 2 * FOX_W
    ff = w_in[:, o:o + FOX_HEADS]; o += FOX_HEADS
    nq = w_in[:, o:o + NSA_W]; o += NSA_W
    nkv = w_in[:, o:o + 6 * NSA_KV_W]; o += 6 * NSA_KV_W
    ng = w_in[:, o:o + N_GATES]
    misc = jnp.concatenate([ff, ng, jnp.zeros((d, LANES - FOX_HEADS - N_GATES), w_in.dtype)], axis=1)
    nq = nq[:, _pair_perm(HEAD_DIM)]
    nk = nkv.reshape(d, 3, 2, NSA_KV_W)[:, :, 0].reshape(d, 3 * NSA_KV_W)
    w_all = jnp.concatenate([fq, fkv, misc, nq, _rot_cols(nq), nkv, _rot_cols(nk)], axis=1).astype(BF16)
    fb = jnp.concatenate([fox_fb.astype(F32), jnp.zeros((LANES - FOX_HEADS,), F32)]).reshape(1, LANES)
    return w_all, fb


def _rope_tables(pos):
    half = HEAD_DIM // 2
    inv = ROPE_THETA ** (-jnp.arange(half, dtype=F32) / half)
    ang = pos.astype(F32)[:, None] * inv[None, :]
    reps = LANES // half
    return jnp.tile(jnp.cos(ang), (1, reps)), jnp.tile(jnp.sin(ang), (1, reps))


def _proj_kernel(x_ref, g_ref, w_ref, fb_ref, cos_ref, sin_ref,
                 fq_ref, fkv_ref, fkvb_ref, misc_ref, nq_ref, nkv_ref, nkvb_ref):
    x = x_ref[...]
    ms = jnp.mean(x * x, axis=-1, keepdims=True)
    hb = ((x * lax.rsqrt(ms + RMS_EPS)) * g_ref[...]).astype(BF16)

    def mm(c0, c1):
        return jnp.dot(hb, w_ref[:, c0:c1], preferred_element_type=F32)

    fq_ref[...] = (mm(_C_FQ, _C_FKV) * ATTN_SCALE).astype(BF16)
    fkv = mm(_C_FKV, _C_MISC)
    fkv_ref[...] = fkv
    fkvb_ref[...] = fkv.astype(BF16)
    z = mm(_C_MISC, _C_NQ) + fb_ref[...]
    lane = lax.broadcasted_iota(jnp.int32, z.shape, 1)
    log_sig = jnp.minimum(z, 0.0) - jnp.log1p(jnp.exp(-jnp.abs(z)))
    misc_ref[...] = jnp.where(lane < FOX_HEADS, log_sig, jax.nn.sigmoid(z))
    cos = cos_ref[...]
    sin = sin_ref[...]
    reps = NSA_W // LANES
    cos_q = jnp.concatenate([cos] * reps, axis=1)
    sin_q = jnp.concatenate([sin] * reps, axis=1)
    nq = mm(_C_NQ, _C_NQR) * cos_q + mm(_C_NQR, _C_NKV) * sin_q
    nq_ref[...] = (nq * ATTN_SCALE).astype(BF16)
    for s in range(6):
        c0 = _C_NKV + s * NSA_KV_W
        v = mm(c0, c0 + NSA_KV_W)
        if s % 2 == 0:
            r0 = _C_NKR + (s // 2) * NSA_KV_W
            v = v * cos + mm(r0, r0 + NSA_KV_W) * sin
        nkv_ref[:, s * NSA_KV_W:(s + 1) * NSA_KV_W] = v
        nkvb_ref[:, s * NSA_KV_W:(s + 1) * NSA_KV_W] = v.astype(BF16)


def _project(x2, g_mix, w_all, fb, cos, sin, tm):
    n, d = x2.shape
    tok = lambda i: (i, 0)
    full = lambda i: (0, 0)
    widths = [(FOX_W, BF16), (2 * FOX_W, F32), (2 * FOX_W, BF16), (LANES, F32), (NSA_W, BF16),
              (6 * NSA_KV_W, F32), (6 * NSA_KV_W, BF16)]
    return pl.pallas_call(
        _proj_kernel,
        grid=(n // tm,),
        in_specs=[pl.BlockSpec((tm, d), tok), pl.BlockSpec((1, d), full), pl.BlockSpec(w_all.shape, full),
                  pl.BlockSpec((1, LANES), full), pl.BlockSpec((tm, LANES), tok), pl.BlockSpec((tm, LANES), tok)],
        out_specs=[pl.BlockSpec((tm, w), tok) for w, _ in widths],
        out_shape=[jax.ShapeDtypeStruct((n, w), dt) for w, dt in widths],
        compiler_params=_cparams("parallel"),
    )(x2, g_mix.reshape(1, d), w_all, fb, cos, sin)


def _merge_kernel(x_ref, of_ref, on_ref, gf_ref, gn_ref, wf_ref, wn_ref, o_ref):
    def norm(v, g):
        v = v.astype(F32)
        ms = jnp.mean(v * v, axis=-1, keepdims=True)
        return ((v * lax.rsqrt(ms + RMS_EPS)) * g).astype(BF16)

    yf = norm(of_ref[...], gf_ref[...])
    yn = norm(on_ref[...], gn_ref[...])
    o_ref[...] = (x_ref[...] + jnp.dot(yf, wf_ref[...], preferred_element_type=F32)
                  + jnp.dot(yn, wn_ref[...], preferred_element_type=F32))


def _merge(x2, of, on, g_fox, g_nsa, w_f, w_n, tm):
    n, d = x2.shape
    tok = lambda i: (i, 0)
    full = lambda i: (0, 0)
    return pl.pallas_call(
        _merge_kernel,
        grid=(n // tm,),
        in_specs=[pl.BlockSpec((tm, d), tok), pl.BlockSpec((tm, FOX_W), tok), pl.BlockSpec((tm, NSA_W), tok),
                  pl.BlockSpec((1, FOX_W), full), pl.BlockSpec((1, NSA_W), full),
                  pl.BlockSpec((FOX_W, d), full), pl.BlockSpec((NSA_W, d), full)],
        out_specs=pl.BlockSpec((tm, d), tok),
        out_shape=jax.ShapeDtypeStruct((n, d), F32),
        compiler_params=_cparams("parallel"),
    )(x2, of, on, g_fox.reshape(1, -1), g_nsa.reshape(1, -1), w_f, w_n)


_NT = (((1,), (1,)), ((), ()))


def _flash_init(m_ref, l_ref, acc_ref):
    m_ref[...] = jnp.full(m_ref.shape, NEG_INF, F32)
    l_ref[...] = jnp.zeros(l_ref.shape, F32)
    acc_ref[...] = jnp.zeros(acc_ref.shape, F32)


def _flash_steps(qhs, load_k, load_v, bias_fn, lo, hi, tk, m_ref, l_ref, acc_ref):
    def body(kt, carry):
        ks = pl.multiple_of(kt * tk, tk)
        k = load_k(ks)
        v = load_v(ks)
        biases = bias_fn(ks)
        for n, qh in enumerate(qhs):
            s = lax.dot_general(qh, k, _NT, preferred_element_type=F32) + biases[n]
            m_prev = m_ref[n]
            m_new = jnp.maximum(m_prev, jnp.max(s, axis=1, keepdims=True))
            m_safe = jnp.where(m_new == NEG_INF, 0.0, m_new)
            p = jnp.exp(s - _lane_repeat(m_safe, tk))
            alpha = jnp.exp(m_prev - m_safe)
            l_ref[n] = alpha * l_ref[n] + jnp.sum(p, axis=1, keepdims=True)
            acc_ref[n] = alpha * acc_ref[n] + jnp.dot(p.astype(BF16), v, preferred_element_type=F32)
            m_ref[n] = m_new
        return carry

    lax.fori_loop(lo, hi, body, 0)


def _lane_repeat(x, width):
    return jnp.concatenate([x] * (width // LANES), axis=1)


def _flash_finish(l_ref, acc_ref):
    return acc_ref[...] / jnp.maximum(l_ref[...], 1e-30)


def _half_mask(shape, hh):
    lane = lax.broadcasted_iota(jnp.int32, shape, 1)
    return (lane >= HEAD_DIM) if hh else (lane < HEAD_DIM)


def _causal_bias(bias, qstart, ks):
    qpos = qstart + lax.broadcasted_iota(jnp.int32, bias.shape, 0)
    kpos = ks + lax.broadcasted_iota(jnp.int32, bias.shape, 1)
    return jnp.where(kpos <= qpos, bias, NEG_INF)


def _fox_prompt_kernel(q_ref, k_ref, v_ref, cq_ref, ck_ref, o_ref, m_ref, l_ref, acc_ref, *, tq):
    j = pl.program_id(1)
    qi = pl.program_id(2)
    qstart = qi * tq
    qp = q_ref[0]
    load_k = lambda ks: k_ref[0, pl.ds(ks, tq), :]
    load_v = lambda ks: v_ref[0, pl.ds(ks, tq), :]
    halves = [_half_mask(qp.shape, hh) for hh in range(2)]
    qhs = [jnp.where(half, qp, jnp.zeros_like(qp)) for half in halves]
    cqs = [_lane_repeat(jnp.broadcast_to(cq_ref[0, 0, :, hh:hh + 1], (tq, LANES)), tq) for hh in range(2)]
    bias = lambda ks: [cqs[hh] - ck_ref[0, pl.ds(2 * j + hh, 1), pl.ds(ks, tq)] for hh in range(2)]
    bias_diag = lambda ks: [_causal_bias(bb, qstart, ks) for bb in bias(ks)]
    _flash_init(m_ref, l_ref, acc_ref)
    _flash_steps(qhs, load_k, load_v, bias, 0, qi, tq, m_ref, l_ref, acc_ref)
    _flash_steps(qhs, load_k, load_v, bias_diag, qi, qi + 1, tq, m_ref, l_ref, acc_ref)
    o = _flash_finish(l_ref, acc_ref)
    o_ref[0] = jnp.where(halves[0], o[0], o[1])


def _fox_prompt(fq, fkvb, c, tq):
    b, t, _ = fq.shape
    n_pairs = FOX_W // LANES
    cq = c.reshape(b, t, n_pairs, 2).transpose(0, 2, 1, 3)
    ck = c.transpose(0, 2, 1)
    return pl.pallas_call(
        functools.partial(_fox_prompt_kernel, tq=tq),
        grid=(b, n_pairs, t // tq),
        in_specs=[pl.BlockSpec((1, tq, LANES), lambda bi, j, qi: (bi, qi, j)),
                  pl.BlockSpec((1, t, LANES), lambda bi, j, qi: (bi, 0, j)),
                  pl.BlockSpec((1, t, LANES), lambda bi, j, qi: (bi, 0, n_pairs + j)),
                  pl.BlockSpec((1, 1, tq, 2), lambda bi, j, qi: (bi, j, qi, 0)),
                  pl.BlockSpec((1, FOX_HEADS, t), lambda bi, j, qi: (bi, 0, 0))],
        out_specs=pl.BlockSpec((1, tq, LANES), lambda bi, j, qi: (bi, qi, j)),
        out_shape=jax.ShapeDtypeStruct((b, t, FOX_W), F32),
        scratch_shapes=[pltpu.VMEM((2, tq, LANES), F32), pltpu.VMEM((2, tq, LANES), F32), pltpu.VMEM((2, tq, LANES), F32)],
        compiler_params=_cparams("parallel", "parallel", "arbitrary"),
    )(fq, fkvb, fkvb, cq, ck)


def _compress_kernel(x_ref, pe_ref, w1_ref, w2_ref, o_ref):
    x = (x_ref[0] + pe_ref[0]).astype(BF16)
    h = _gelu(jnp.dot(x, w1_ref[0], preferred_element_type=F32))
    o_ref[0] = jnp.dot(h.astype(BF16), w2_ref[0], preferred_element_type=F32)


def _compress(xw, pe, w1, w2, tr):
    _, r, kd = xw.shape
    return pl.pallas_call(
        _compress_kernel,
        grid=(2, r // tr),
        in_specs=[pl.BlockSpec((1, tr, kd), lambda s, i: (s, i, 0)),
                  pl.BlockSpec((1, 1, kd), lambda s, i: (s, 0, 0)),
                  pl.BlockSpec((1, kd, HEAD_DIM), lambda s, i: (s, 0, 0)),
                  pl.BlockSpec((1, HEAD_DIM, HEAD_DIM), lambda s, i: (s, 0, 0))],
        out_specs=pl.BlockSpec((1, tr, HEAD_DIM), lambda s, i: (s, i, 0)),
        out_shape=jax.ShapeDtypeStruct((2, r, HEAD_DIM), F32),
        compiler_params=_cparams("parallel", "parallel"),
    )(xw, pe.reshape(2, 1, kd), w1.astype(BF16), w2.astype(BF16))


def _cmp_windows(rows):
    two, b, length, g, d = rows.shape
    nch = length // CMP_STRIDE
    ch = rows[:, :, :nch * CMP_STRIDE].reshape(two, b, nch, CMP_STRIDE, g, d).transpose(0, 1, 2, 4, 3, 5)
    ch = ch.reshape(two, b, nch, g, CMP_STRIDE * d)
    reps = CMP_LEN // CMP_STRIDE
    nc = nch - reps + 1
    return jnp.concatenate([ch[:, :, r:r + nc] for r in range(reps)], axis=-1)


def _round_up(x, m):
    return -(-x // m) * m


def _compress_rows(rows, pe, w1, w2):
    xw = _cmp_windows(rows)
    two, b, nc, g, kd = xw.shape
    r = b * nc * g
    tr = min(512, _round_up(r, 8))
    rp = _round_up(r, tr)
    xw = jnp.pad(xw.reshape(2, r, kd), ((0, 0), (0, rp - r), (0, 0)))
    return _compress(xw, pe, w1, w2, tr)[:, :r].reshape(2, b, nc, g, HEAD_DIM)


def _sel_cover_t(nc, nb, ncp):
    cs = np.arange(ncp)[None, :] * CMP_STRIDE
    bs = np.arange(nb)[:, None] * SEL_BLOCK
    cov = np.clip(np.minimum(cs + CMP_LEN, bs + SEL_BLOCK) - np.maximum(cs, bs), 0, None)
    cov = np.where(np.arange(ncp)[None, :] < nc, cov, 0)
    return jnp.asarray(cov.astype(np.float32) / CMP_LEN, BF16)


def _nsa_prompt_kernel(q_ref, kv_ref, kc_ref, vc_ref, misc_ref, covt_ref, o_ref,
                       qh_ref, oc_ref, bias_ref, m_ref, l_ref, acc_ref, *, tq, nc, nb, n_sel):
    qi = pl.program_id(1)
    qstart = qi * tq
    ncp = kc_ref.shape[1]
    n_pairs = NSA_W // LANES
    load = lambda slot: (lambda ks: kv_ref[0, pl.ds(ks, tq), slot * LANES:(slot + 1) * LANES])

    cidx = lax.broadcasted_iota(jnp.int32, (tq, ncp), 1)
    qpos_c = qstart + lax.broadcasted_iota(jnp.int32, (tq, ncp), 0)
    cmask = (cidx * CMP_STRIDE + (CMP_LEN - 1) <= qpos_c) & (cidx < nc)
    psum = [jnp.zeros((tq, ncp), F32) for _ in range(NSA_KV)]
    for j in range(n_pairs):
        qp = q_ref[0, :, j * LANES:(j + 1) * LANES]
        for g in range(NSA_KV):
            h = j * NSA_KV + g
            qh = jnp.where(_half_mask(qp.shape, g), qp, jnp.zeros_like(qp))
            qh_ref[h] = qh
            s = lax.dot_general(qh, kc_ref[0], _NT, preferred_element_type=F32)
            s = jnp.where(cmask, s, NEG_INF)
            m = jnp.max(s, axis=1, keepdims=True)
            m = jnp.where(m == NEG_INF, 0.0, m)
            e = jnp.exp(s - m)
            p = e / jnp.maximum(jnp.sum(e, axis=1, keepdims=True), 1e-30)
            psum[g] = psum[g] + p
            oc_ref[h] = jnp.dot(p.astype(BF16), vc_ref[0], preferred_element_type=F32)

    for g in range(NSA_KV):
        hi = psum[g].astype(BF16)
        lo = (psum[g] - hi.astype(F32)).astype(BF16)
        pb = (lax.dot_general(covt_ref[...], hi, _NT, preferred_element_type=F32)
              + lax.dot_general(covt_ref[...], lo, _NT, preferred_element_type=F32))
        jj = lax.broadcasted_iota(jnp.int32, (nb, tq), 0)
        cur = (qstart + lax.broadcasted_iota(jnp.int32, (nb, tq), 1)) // SEL_BLOCK
        causal = jj <= cur
        forced = (jj == 0) | (jj == cur) | (jj == cur - 1)
        score = jnp.where(forced, FORCED_SCORE, jnp.where(causal, pb, NEG_INF))
        _, _, rank = _extract_topk(score, n_sel)
        sel_t = jnp.where((rank < n_sel) & causal, 1.0, 0.0)
        sel_t = jnp.concatenate([sel_t, jnp.zeros((LANES - nb, tq), F32)], axis=0)
        sel = sel_t.T.astype(BF16)

        def fill(kt, carry):
            ks = pl.multiple_of(kt * tq, tq)
            blk = (ks + lax.broadcasted_iota(jnp.int32, (LANES, tq), 1)) // SEL_BLOCK
            expand = jnp.where(lax.broadcasted_iota(jnp.int32, (LANES, tq), 0) == blk, 1.0, 0.0).astype(BF16)
            hit = jnp.dot(sel, expand, preferred_element_type=F32)
            bias = jnp.where(hit > 0.5, 0.0, NEG_INF)
            bias_ref[g, :, pl.ds(ks, tq)] = _causal_bias(bias, qstart, ks)
            return carry

        lax.fori_loop(0, qi + 1, fill, 0)

    def win_bias(ks):
        qpos = qstart + lax.broadcasted_iota(jnp.int32, (tq, tq), 0)
        kpos = ks + lax.broadcasted_iota(jnp.int32, (tq, tq), 1)
        dist = qpos - kpos
        return jnp.where((dist >= 0) & (dist < WINDOW), 0.0, NEG_INF)

    win_lo = jnp.maximum(qi - (-(-WINDOW // tq)), 0)
    sel_bias = lambda ks: [bias_ref[g, :, pl.ds(ks, tq)] for g in range(NSA_KV)]
    win_bias2 = lambda ks: [win_bias(ks)] * NSA_KV
    for j in range(n_pairs):
        qhs = [qh_ref[j * NSA_KV + g] for g in range(NSA_KV)]
        _flash_init(m_ref, l_ref, acc_ref)
        _flash_steps(qhs, load(2), load(3), sel_bias, 0, qi + 1, tq, m_ref, l_ref, acc_ref)
        o_s = _flash_finish(l_ref, acc_ref)
        _flash_init(m_ref, l_ref, acc_ref)
        _flash_steps(qhs, load(4), load(5), win_bias2, win_lo, qi + 1, tq, m_ref, l_ref, acc_ref)
        o_w = _flash_finish(l_ref, acc_ref)
        outs = []
        for g in range(NSA_KV):
            col = FOX_HEADS + (g * NSA_HPG + j) * 3
            gates = [misc_ref[0, :, col + r:col + r + 1] for r in range(3)]
            outs.append(gates[0] * oc_ref[j * NSA_KV + g] + gates[1] * o_s[g] + gates[2] * o_w[g])
        o_ref[0, :, j * LANES:(j + 1) * LANES] = jnp.where(_half_mask(outs[0].shape, 0), outs[0], outs[1])


def _nsa_prompt(nq, nkvb, kc, vc, misc, tq):
    b, t, _ = nq.shape
    nc = kc.shape[1]
    ncp = _round_up(nc, LANES)
    nb = -(-t // SEL_BLOCK)
    assert nb <= LANES and t % tq == 0 and tq % SEL_BLOCK == 0
    pad = ((0, 0), (0, ncp - nc), (0, 0))
    kcp = jnp.pad(kc, pad).astype(BF16)
    vcp = jnp.pad(vc, pad).astype(BF16)
    covt = _sel_cover_t(nc, nb, ncp)
    kern = functools.partial(_nsa_prompt_kernel, tq=tq, nc=nc, nb=nb, n_sel=min(N_SELECT, nb))
    seq = lambda bi, qi: (bi, 0, 0)
    tile = lambda bi, qi: (bi, qi, 0)
    return pl.pallas_call(
        kern,
        grid=(b, t // tq),
        in_specs=[pl.BlockSpec((1, tq, NSA_W), tile), pl.BlockSpec((1, t, 6 * NSA_KV_W), seq),
                  pl.BlockSpec((1, ncp, LANES), seq), pl.BlockSpec((1, ncp, LANES), seq),
                  pl.BlockSpec((1, tq, LANES), tile), pl.BlockSpec((nb, ncp), lambda bi, qi: (0, 0))],
        out_specs=pl.BlockSpec((1, tq, NSA_W), tile),
        out_shape=jax.ShapeDtypeStruct((b, t, NSA_W), F32),
        scratch_shapes=[pltpu.VMEM((NSA_HEADS, tq, LANES), BF16), pltpu.VMEM((NSA_HEADS, tq, LANES), F32),
                        pltpu.VMEM((NSA_KV, tq, t), F32),
                        pltpu.VMEM((NSA_KV, tq, LANES), F32), pltpu.VMEM((NSA_KV, tq, LANES), F32),
                        pltpu.VMEM((NSA_KV, tq, LANES), F32)],
        compiler_params=_cparams("parallel", "arbitrary"),
    )(nq, nkvb, kcp, vcp, misc, covt)


def _tile(n, pref):
    t = min(pref, n)
    while n % t:
        t //= 2
    return t


def _mixer_prompt(x, g_mix, w_all, fb, cmp_pe, cmp_w1, cmp_w2):
    b, t, d = x.shape
    n = b * t
    cos, sin = _rope_tables(jnp.arange(t))
    cos = jnp.tile(cos, (b, 1))
    sin = jnp.tile(sin, (b, 1))
    fq, fkv, fkvb, misc, nq, nkv, nkvb = _project(x.reshape(n, d), g_mix, w_all, fb, cos, sin, _tile(n, 512))
    logf = misc[:, :FOX_HEADS].reshape(b, t, FOX_HEADS)
    c = jnp.cumsum(logf, axis=1)
    tq = _tile(t, 512)
    of = _fox_prompt(fq.reshape(b, t, FOX_W), fkvb.reshape(b, t, 2 * FOX_W), c, tq)
    rows = nkv.reshape(b, t, 6, NSA_KV, HEAD_DIM)
    kvc = _compress_rows(jnp.stack([rows[:, :, 0], rows[:, :, 1]]), cmp_pe, cmp_w1, cmp_w2)
    nc = kvc.shape[2]
    on = _nsa_prompt(nq.reshape(b, t, NSA_W), nkvb.reshape(b, t, 6 * NSA_KV_W),
                     kvc[0].reshape(b, nc, NSA_KV_W), kvc[1].reshape(b, nc, NSA_KV_W),
                     misc.reshape(b, t, LANES), tq)
    wb = min(WINDOW, t)
    return (of, on, fkv.reshape(b, t, 2, FOX_HEADS, HEAD_DIM), logf, rows[:, :, :4], rows[:, t - wb:, 4:])


def _split3(x):
    hi = x.astype(BF16)
    r = x - hi.astype(F32)
    mid = r.astype(BF16)
    return hi, mid, (r - mid.astype(F32)).astype(BF16)


def _split2(x):
    hi = x.astype(BF16)
    return hi, (x - hi.astype(F32)).astype(BF16)


def _softmax_update(s_list, v_list, m_ref, l_ref, acc_ref, v_transposed=False):
    pv = (lambda p, v: lax.dot_general(p, v, _NT, preferred_element_type=F32)) if v_transposed else (
        lambda p, v: jnp.dot(p, v, preferred_element_type=F32))
    m_prev = m_ref[...]
    m_new = m_prev
    for s in s_list:
        m_new = jnp.maximum(m_new, jnp.max(s, axis=1, keepdims=True))
    m_safe = jnp.where(m_new == NEG_INF, 0.0, m_new)
    alpha = jnp.exp(m_prev - m_safe)
    l_new = alpha * l_ref[...]
    acc = alpha * acc_ref[...]
    for s, v_bf in zip(s_list, v_list):
        p = jnp.exp(s - m_safe)
        l_new = l_new + jnp.sum(p, axis=1, keepdims=True)
        acc = acc + pv(p.astype(BF16), v_bf)
    l_ref[...] = l_new
    acc_ref[...] = acc
    m_ref[...] = m_new


def _fox_sample_kernel(pt_ref, q_ref, cnew_ref, knew_ref, vnew_ref, bnew_ref, scan_ref, *rest, n_pg, s_tok):
    kv_refs, lf_refs = rest[:n_pg], rest[n_pg:2 * n_pg]
    o_ref, m_ref, l_ref, acc_ref, carry_ref = rest[2 * n_pg:]
    st = pl.program_id(1)

    @pl.when(st == 0)
    def _():
        _flash_init(m_ref, l_ref, acc_ref)
        carry_ref[...] = jnp.zeros(carry_ref.shape, F32)

    q = q_ref[0]
    cnew = cnew_ref[0]
    page_rows = kv_refs[0].shape[2]
    carry = carry_ref[...]
    s_list, v_list = [], []
    for i in reversed(range(n_pg)):
        x = lf_refs[i][...]
        sc = sum(jnp.dot(p, scan_ref[...], preferred_element_type=F32) for p in _split3(x))
        d = sc[:, :page_rows] + carry
        carry = carry + sc[:, page_rows:]
        dd = jnp.concatenate([jnp.broadcast_to(d[h:h + 1], (s_tok, page_rows)) for h in range(FOX_HEADS)], axis=0)
        s_list.append(jnp.dot(q, kv_refs[i][0].astype(BF16), preferred_element_type=F32) + (cnew + dd))
        v_list.append(kv_refs[i][1].astype(BF16))
    carry_ref[...] = carry
    _softmax_update(s_list, v_list, m_ref, l_ref, acc_ref, v_transposed=True)

    @pl.when(st == pl.num_programs(1) - 1)
    def _():
        s = lax.dot_general(q, knew_ref[0], _NT, preferred_element_type=F32) + bnew_ref[0]
        _softmax_update([s], [vnew_ref[0]], m_ref, l_ref, acc_ref)
        o = _flash_finish(l_ref, acc_ref)
        lane_head = lax.broadcasted_iota(jnp.int32, (s_tok, FOX_W), 1) // HEAD_DIM
        out = jnp.zeros((s_tok, FOX_W), F32)
        for h in range(FOX_HEADS):
            out = jnp.where(lane_head == h, o[h * s_tok:(h + 1) * s_tok], out)
        o_ref[0] = out


def _fox_sample(layer, page_table, q_rows, cnew_b, knew, vnew, bnew, cache_kv, logf_t, n_pg):
    bd, n_pages = page_table.shape
    page_rows = cache_kv.shape[4]
    s_tok = q_rows.shape[1] // FOX_HEADS
    rows = q_rows.shape[1]
    kidx = np.arange(page_rows)
    scan = np.concatenate([(kidx[:, None] > kidx[None, :]), np.ones((page_rows, page_rows), bool)], axis=1)
    scan = jnp.asarray(scan, BF16)
    n_st = n_pages // n_pg
    seq = lambda b, st, pt: (b, 0, 0)

    def page_map(i, nd):
        return lambda b, st, pt: (layer, pt[b, n_pages - (st + 1) * n_pg + i]) + (0,) * nd

    grid_spec = pltpu.PrefetchScalarGridSpec(
        num_scalar_prefetch=1,
        grid=(bd, n_st),
        in_specs=[pl.BlockSpec((1, rows, FOX_W), seq), pl.BlockSpec((1, rows, page_rows), seq),
                  pl.BlockSpec((1, page_rows, FOX_W), seq), pl.BlockSpec((1, page_rows, FOX_W), seq),
                  pl.BlockSpec((1, rows, page_rows), seq),
                  pl.BlockSpec(scan.shape, lambda b, st, pt: (0, 0))]
                 + [pl.BlockSpec((None, None, 2, FOX_W, page_rows), page_map(i, 3)) for i in range(n_pg)]
                 + [pl.BlockSpec((None, None, FOX_HEADS, page_rows), page_map(i, 2)) for i in range(n_pg)],
        out_specs=pl.BlockSpec((1, s_tok, FOX_W), seq),
        scratch_shapes=[pltpu.VMEM((rows, 1), F32), pltpu.VMEM((rows, 1), F32), pltpu.VMEM((rows, FOX_W), F32),
                        pltpu.VMEM((FOX_HEADS, page_rows), F32)])
    return pl.pallas_call(
        functools.partial(_fox_sample_kernel, n_pg=n_pg, s_tok=s_tok),
        grid_spec=grid_spec,
        out_shape=jax.ShapeDtypeStruct((bd, s_tok, FOX_W), F32),
        compiler_params=_cparams("parallel", "arbitrary"),
    )(page_table, q_rows, cnew_b, knew, vnew, bnew, scan, *([cache_kv] * n_pg), *([logf_t] * n_pg))


def _nsa_cmp_sample_kernel(pt_ref, pe_ref, w1_ref, w2_ref, *rest, n_pg):
    pages = rest[:n_pg]
    kc_ref, vc_ref, carry_ref, xs_ref = rest[n_pg:]
    st = pl.program_id(1)

    @pl.when(st == 0)
    def _():
        carry_ref[...] = jnp.zeros(carry_ref.shape, F32)

    page_rows = pages[0].shape[2]
    m = n_pg * page_rows // CMP_STRIDE
    last = lax.broadcasted_iota(jnp.int32, (m, NSA_KV_W), 0) == m - 1
    for slot, out_ref in ((0, kc_ref), (1, vc_ref)):
        for i, pg in enumerate(pages):
            xs_ref[i * page_rows:(i + 1) * page_rows, :] = pg[slot].T
        a = jnp.zeros((m, NSA_KV_W), F32)
        b = jnp.zeros((m, NSA_KV_W), F32)
        for r in range(CMP_STRIDE):
            x = xs_ref[pl.ds(r, m, stride=CMP_STRIDE), :]
            xa = (x + pe_ref[slot, r:r + 1, :]).astype(BF16)
            xb = (x + pe_ref[slot, CMP_STRIDE + r:CMP_STRIDE + r + 1, :]).astype(BF16)
            a = a + jnp.dot(xa, w1_ref[slot, r], preferred_element_type=F32)
            b = b + jnp.dot(xb, w1_ref[slot, CMP_STRIDE + r], preferred_element_type=F32)
        b_next = jnp.where(last, carry_ref[slot, 0:1, :], pltpu.roll(b, m - 1, axis=0))
        carry_ref[slot] = b
        h = _gelu(a + b_next).astype(BF16)
        out_ref[0] = jnp.dot(h, w2_ref[slot], preferred_element_type=F32).astype(BF16)


def _nsa_cmp_sample(layer, page_table, cache_nsa, pe2, w1_bd, w2_bd, n_pg):
    bd, n_pages = page_table.shape
    page_rows = cache_nsa.shape[4]
    cpp = page_rows // CMP_STRIDE
    m = n_pg * cpp
    n_st = n_pages // n_pg
    const = lambda nd: (lambda b, st, pt: (0,) * nd)

    def page_map(i):
        return lambda b, st, pt: (layer, pt[b, n_pages - (st + 1) * n_pg + i], 0, 0, 0)

    out_spec = pl.BlockSpec((1, m, NSA_KV_W), lambda b, st, pt: (b, n_st - 1 - st, 0))
    grid_spec = pltpu.PrefetchScalarGridSpec(
        num_scalar_prefetch=1,
        grid=(bd, n_st),
        in_specs=[pl.BlockSpec(pe2.shape, const(3)), pl.BlockSpec(w1_bd.shape, const(4)),
                  pl.BlockSpec(w2_bd.shape, const(3))]
                 + [pl.BlockSpec((None, None) + cache_nsa.shape[2:], page_map(i)) for i in range(n_pg)],
        out_specs=[out_spec, out_spec],
        scratch_shapes=[pltpu.VMEM((2, m, NSA_KV_W), F32), pltpu.VMEM((n_pg * page_rows, NSA_KV_W), F32)])
    shape = jax.ShapeDtypeStruct((bd, n_pages * cpp, NSA_KV_W), BF16)
    return pl.pallas_call(
        functools.partial(_nsa_cmp_sample_kernel, n_pg=n_pg),
        grid_spec=grid_spec,
        out_shape=[shape, shape],
        compiler_params=_cparams("parallel", "arbitrary"),
    )(page_table, pe2, w1_bd, w2_bd, *([cache_nsa] * n_pg))


def _nsa_sample_kernel(pt_ref, q_ref, kc_ref, vc_ref, gate_ref, covt_ref, gsum_ref, win_ref, new_ref, *rest,
                       n_pg, s_tok, nc, nb, n_sel, past):
    pages = rest[:n_pg]
    o_ref, bias_ref, oc_ref, m_ref, l_ref, acc_ref = rest[n_pg:]
    st = pl.program_id(1)
    q = q_ref[0]
    rows = q.shape[0]
    page_rows = pages[0].shape[2]

    @pl.when(st == 0)
    def _():
        ncp = kc_ref.shape[1]
        nbp = covt_ref.shape[0]
        s = lax.dot_general(q, kc_ref[0], _NT, preferred_element_type=F32)
        cidx = lax.broadcasted_iota(jnp.int32, (rows, ncp), 1)
        qpos = past + lax.broadcasted_iota(jnp.int32, (rows, ncp), 0) % s_tok
        s = jnp.where((cidx * CMP_STRIDE + (CMP_LEN - 1) <= qpos) & (cidx < nc), s, NEG_INF)
        m = jnp.max(s, axis=1, keepdims=True)
        m = jnp.where(m == NEG_INF, 0.0, m)
        e = jnp.exp(s - m)
        p = e / jnp.maximum(jnp.sum(e, axis=1, keepdims=True), 1e-30)
        oc_ref[...] = jnp.dot(p.astype(BF16), vc_ref[0], preferred_element_type=F32)
        psum = sum(jnp.dot(gsum_ref[...], x, preferred_element_type=F32) for x in _split2(p))
        pb = sum(lax.dot_general(covt_ref[...], x, _NT, preferred_element_type=F32) for x in _split2(psum))
        jj = lax.broadcasted_iota(jnp.int32, (nbp, rows), 0)
        cur = (past + lax.broadcasted_iota(jnp.int32, (nbp, rows), 1) % s_tok) // SEL_BLOCK
        causal = (jj <= cur) & (jj < nb)
        forced = (jj == 0) | (jj == cur) | (jj == cur - 1)
        score = jnp.where(forced, FORCED_SCORE, jnp.where(causal, pb, NEG_INF))
        _, _, rank = _extract_topk(score, n_sel)
        sel = jnp.where((rank < n_sel) & causal, 1.0, 0.0).T.astype(BF16)
        chunk = min(past, 16 * page_rows)

        def fill(c, carry):
            ks = pl.multiple_of(c * chunk, chunk)
            blk = (ks + lax.broadcasted_iota(jnp.int32, (nbp, chunk), 1)) // SEL_BLOCK
            expand = jnp.where(lax.broadcasted_iota(jnp.int32, (nbp, chunk), 0) == blk, 1.0, 0.0).astype(BF16)
            hit = jnp.dot(sel, expand, preferred_element_type=F32)
            bias_ref[:, pl.ds(ks, chunk)] = jnp.where(hit > 0.5, 0.0, NEG_INF)
            return carry

        lax.fori_loop(0, bias_ref.shape[1] // chunk, fill, 0)
        _flash_init(m_ref, l_ref, acc_ref)

    s_list, v_list = [], []
    for i in range(n_pg):
        ks = pl.multiple_of((st * n_pg + i) * page_rows, page_rows)
        s = jnp.dot(q, pages[i][2].astype(BF16), preferred_element_type=F32)
        s_list.append(s + bias_ref[:, pl.ds(ks, page_rows)])
        v_list.append(pages[i][3].astype(BF16))
    _softmax_update(s_list, v_list, m_ref, l_ref, acc_ref, v_transposed=True)

    @pl.when(st == pl.num_programs(1) - 1)
    def _():
        tok = lax.broadcasted_iota(jnp.int32, (rows, page_rows), 0) % s_tok
        col = lax.broadcasted_iota(jnp.int32, (rows, page_rows), 1)
        bnew = jnp.where((col <= tok) & (col < s_tok), 0.0, NEG_INF)
        s = lax.dot_general(q, new_ref[0, 0], _NT, preferred_element_type=F32) + bnew
        _softmax_update([s], [new_ref[0, 1]], m_ref, l_ref, acc_ref)
        o_s = _flash_finish(l_ref, acc_ref)
        wb = win_ref.shape[2]
        wtok = lax.broadcasted_iota(jnp.int32, (rows, wb), 0) % s_tok
        wi = lax.broadcasted_iota(jnp.int32, (rows, wb), 1)
        dist = wb + wtok - wi
        ok = (past - wb + wi >= 0) & (dist >= 0) & (dist < WINDOW)
        s1 = jnp.dot(q, win_ref[0].astype(BF16), preferred_element_type=F32)
        s1 = jnp.where(ok, s1, NEG_INF)
        s2 = lax.dot_general(q, new_ref[0, 2], _NT, preferred_element_type=F32) + bnew
        m = jnp.maximum(jnp.max(s1, axis=1, keepdims=True), jnp.max(s2, axis=1, keepdims=True))
        e1 = jnp.exp(s1 - m)
        e2 = jnp.exp(s2 - m)
        den = jnp.sum(e1, axis=1, keepdims=True) + jnp.sum(e2, axis=1, keepdims=True)
        o_w = (lax.dot_general(e1.astype(BF16), win_ref[1].astype(BF16), _NT, preferred_element_type=F32)
               + jnp.dot(e2.astype(BF16), new_ref[0, 3], preferred_element_type=F32)) / den
        o = gate_ref[0, 0] * oc_ref[...] + gate_ref[0, 1] * o_s + gate_ref[0, 2] * o_w
        for j in range(NSA_W // LANES):
            lo = o[(2 * j) * s_tok:(2 * j + 1) * s_tok]
            hi = o[(2 * j + 1) * s_tok:(2 * j + 2) * s_tok]
            o_ref[0, :, j * LANES:(j + 1) * LANES] = jnp.where(_half_mask(lo.shape, 0), lo, hi)


def _nsa_sample(layer, page_table, q_rows, kc, vc, gates, win_state, new_kv, cache_nsa, s_tok, n_pg):
    bd, n_pages = page_table.shape
    page_rows = cache_nsa.shape[4]
    past = n_pages * page_rows
    rows = q_rows.shape[1]
    ncp = kc.shape[1]
    nc = ncp - 1
    nb = -(-(past + s_tok) // SEL_BLOCK)
    nbp = _round_up(nb, LANES)
    assert past % SEL_BLOCK == 0 and s_tok <= SEL_BLOCK and s_tok < CMP_STRIDE
    covt = _sel_cover_t(nc, nbp, ncp)
    r = np.arange(rows)
    same = (r[:, None] % s_tok == r[None, :] % s_tok) & ((r[:, None] // s_tok) % NSA_KV == (r[None, :] // s_tok) % NSA_KV)
    live = r < NSA_HEADS * s_tok
    gsum = jnp.asarray(same & live[:, None] & live[None, :], BF16)
    wb = win_state.shape[4]
    n_st = n_pages // n_pg
    seq = lambda b, st, pt: (b, 0, 0)
    const2 = lambda b, st, pt: (0, 0)
    grid_spec = pltpu.PrefetchScalarGridSpec(
        num_scalar_prefetch=1,
        grid=(bd, n_st),
        in_specs=[pl.BlockSpec((1, rows, LANES), seq), pl.BlockSpec((1, ncp, LANES), seq),
                  pl.BlockSpec((1, ncp, LANES), seq), pl.BlockSpec((1, 3, rows, LANES), lambda b, st, pt: (b, 0, 0, 0)),
                  pl.BlockSpec(covt.shape, const2), pl.BlockSpec(gsum.shape, const2),
                  pl.BlockSpec((None, None, 2, NSA_KV_W, wb), lambda b, st, pt: (layer, b, 0, 0, 0)),
                  pl.BlockSpec((1, 4, page_rows, LANES), lambda b, st, pt: (b, 0, 0, 0))]
                 + [pl.BlockSpec((None, None) + cache_nsa.shape[2:],
                                 (lambda i: lambda b, st, pt: (layer, pt[b, st * n_pg + i], 0, 0, 0))(i))
                    for i in range(n_pg)],
        out_specs=pl.BlockSpec((1, s_tok, NSA_W), seq),
        scratch_shapes=[pltpu.VMEM((rows, past), F32), pltpu.VMEM((rows, LANES), F32),
                        pltpu.VMEM((rows, 1), F32), pltpu.VMEM((rows, 1), F32), pltpu.VMEM((rows, LANES), F32)])
    kern = functools.partial(_nsa_sample_kernel, n_pg=n_pg, s_tok=s_tok, nc=nc, nb=nb, n_sel=min(N_SELECT, nb),
                             past=past)
    return pl.pallas_call(
        kern,
        grid_spec=grid_spec,
        out_shape=jax.ShapeDtypeStruct((bd, s_tok, NSA_W), F32),
        compiler_params=_cparams("parallel", "arbitrary"),
    )(page_table, q_rows, kc, vc, gates, covt, gsum, win_state, new_kv, *([cache_nsa] * n_pg))


def _block_diag_groups(w):
    eye = jnp.eye(NSA_KV, dtype=w.dtype)
    out = jnp.einsum('gh,...de->...gdhe', eye, w)
    return out.reshape(w.shape[:-2] + (NSA_KV_W, NSA_KV_W))


def _pad_rows(x, rows, value=0.0):
    pad = [(0, 0)] * x.ndim
    pad[-2] = (0, rows - x.shape[-2])
    return jnp.pad(x, pad, constant_values=value)


def _mixer_sample(x, layer, page_table, cache_fox_kv, logf_t, cache_nsa_kv, state_win, state_win_rows, g_mix, w_all,
                  fb, cmp_pe, cmp_w1, cmp_w2):
    bd, s, d = x.shape
    n_pages = page_table.shape[1]
    page_rows = cache_fox_kv.shape[4]
    past = n_pages * page_rows
    n = bd * s
    pos = past + jnp.arange(s)
    cos, sin = _rope_tables(pos)
    cos = jnp.tile(cos, (bd, 1))
    sin = jnp.tile(sin, (bd, 1))
    fq, fkv, fkvb, misc, nq, nkv, nkvb = _project(x.reshape(n, d), g_mix, w_all, fb, cos, sin, _tile(n, 256))
    logf = misc[:, :FOX_HEADS].reshape(bd, s, FOX_HEADS)
    cnew = jnp.cumsum(logf, axis=1).transpose(0, 2, 1)
    rows_f = FOX_HEADS * s
    cnew_b = jnp.broadcast_to(cnew.reshape(bd, rows_f, 1), (bd, rows_f, page_rows))
    causal = jnp.arange(s)[None, :] <= jnp.arange(s)[:, None]
    bnew = jnp.where(causal, cnew[:, :, :, None] - cnew[:, :, None, :], NEG_INF).reshape(bd, rows_f, s)
    bnew = jnp.pad(bnew, ((0, 0), (0, 0), (0, page_rows - s)), constant_values=NEG_INF)
    head_of_lane = jnp.arange(FOX_W) // HEAD_DIM
    fq3 = fq.reshape(bd, 1, s, FOX_W)
    q_rows = jnp.where(head_of_lane[None, None, None, :] == jnp.arange(FOX_HEADS)[None, :, None, None], fq3,
                       jnp.zeros_like(fq3)).reshape(bd, rows_f, FOX_W)
    fkvb3 = fkvb.reshape(bd, s, 2 * FOX_W)
    knew = _pad_rows(fkvb3[:, :, :FOX_W], page_rows)
    vnew = _pad_rows(fkvb3[:, :, FOX_W:], page_rows)
    of = _fox_sample(layer, page_table, q_rows, cnew_b, knew, vnew, bnew, cache_fox_kv, logf_t, _tile(n_pages, 8))
    pe2 = jnp.tile(cmp_pe, (1, 1, NSA_KV))
    w1_bd = _block_diag_groups(cmp_w1.reshape(2, CMP_LEN, HEAD_DIM, HEAD_DIM)).astype(BF16)
    w2_bd = _block_diag_groups(cmp_w2).astype(BF16)
    n_pg = _tile(n_pages, 16)
    kc, vc = _nsa_cmp_sample(layer, page_table, cache_nsa_kv, pe2, w1_bd, w2_bd, n_pg)
    rows_n = _round_up(NSA_HEADS * s, LANES)
    nq4 = nq.reshape(bd, s, NSA_W // LANES, 1, LANES).transpose(0, 2, 3, 1, 4)
    half_of_lane = (jnp.arange(LANES) // HEAD_DIM)[None, None, None, None, :]
    qn = jnp.where(half_of_lane == jnp.arange(NSA_KV)[None, None, :, None, None], nq4, jnp.zeros_like(nq4))
    qn = _pad_rows(qn.reshape(bd, NSA_HEADS * s, LANES), rows_n)
    gate = misc[:, FOX_HEADS:FOX_HEADS + N_GATES].reshape(bd, s, NSA_KV, NSA_HPG, 3)
    gate = gate.transpose(0, 4, 3, 2, 1).reshape(bd, 3, NSA_HEADS * s, 1)
    gates = jnp.broadcast_to(_pad_rows(gate, rows_n), (bd, 3, rows_n, LANES))
    nkvb3 = nkvb.reshape(bd, s, 6, NSA_KV_W)
    new_kv = _pad_rows(nkvb3[:, :, 2:6].transpose(0, 2, 1, 3), page_rows)
    on = _nsa_sample(layer, page_table, qn, kc, vc, gates, state_win, new_kv, cache_nsa_kv, s, n_pg)
    rows = nkv.reshape(bd, s, 6, NSA_KV, HEAD_DIM)
    win_all = jnp.concatenate([state_win_rows, rows[:, :, 4:]], axis=1)
    return of, on, fkv.reshape(bd, s, 2, FOX_HEADS, HEAD_DIM), logf, rows[:, :, :4], win_all[:, s:]


def _rmsnorm_kernel(x_ref, g_ref, o_ref):
    x = x_ref[...]
    ms = jnp.mean(x * x, axis=-1, keepdims=True)
    o_ref[...] = (x * lax.rsqrt(ms + RMS_EPS)) * g_ref[...]


def _rmsnorm(x2, g, tm):
    n, d = x2.shape
    return pl.pallas_call(
        _rmsnorm_kernel,
        grid=(n // tm,),
        in_specs=[pl.BlockSpec((tm, d), lambda i: (i, 0)), pl.BlockSpec((1, d), lambda i: (0, 0))],
        out_specs=pl.BlockSpec((tm, d), lambda i: (i, 0)),
        out_shape=jax.ShapeDtypeStruct((n, d), F32),
        compiler_params=_cparams("parallel"),
    )(x2, g.reshape(1, d))


def kernel(x_prompt, x_sample, cache_fox_kv, cache_fox_logf, cache_nsa_kv, state_win_kv, page_table,
           norm_mix, w_in, fox_fb, cmp_pe, cmp_w1, cmp_w2, norm_fox, norm_nsa, w_out,
           norm_ffn, peer_wq, peer_subkeys, peer_u, peer_v, norm_final):
    depth = w_in.shape[0]
    b, t, d = x_prompt.shape
    bd, s, _ = x_sample.shape
    n_seq, n_pages = page_table.shape
    past = n_pages * cache_fox_kv.shape[2]
    np_, ns = b * t, bd * s
    xp = x_prompt.reshape(np_, d)
    xs = x_sample.reshape(ns, d)
    perm = _pair_perm(HEAD_DIM)
    outs = [[] for _ in range(8)]
    pool, page_rows = cache_fox_kv.shape[1:3]
    row_minor = (0, 1, 3, 4, 5, 2)
    fox_kv2 = cache_fox_kv.transpose(row_minor).reshape(depth, pool, 2, FOX_W, page_rows)
    logf_t = cache_fox_logf.transpose(0, 1, 3, 2)
    nsa_kv2 = cache_nsa_kv.transpose(row_minor).reshape(depth, pool, 4, NSA_KV_W, page_rows)
    win2 = state_win_kv.transpose(row_minor).reshape(depth, bd, 2, NSA_KV_W, state_win_kv.shape[2])
    for l in range(depth):
        w_all, fb = _prep_w_in(w_in[l], fox_fb[l])
        mp = _mixer_prompt(xp.reshape(b, t, d), norm_mix[l], w_all, fb, cmp_pe[l], cmp_w1[l], cmp_w2[l])
        msm = _mixer_sample(xs.reshape(bd, s, d), l, page_table, fox_kv2, logf_t, nsa_kv2, win2, state_win_kv[l],
                            norm_mix[l], w_all, fb, cmp_pe[l], cmp_w1[l], cmp_w2[l])
        w_f = w_out[l, :FOX_W].astype(BF16)
        w_n = w_out[l, FOX_W:][perm].astype(BF16)
        g_n = norm_nsa[l][perm]
        xp = _merge(xp, mp[0].reshape(np_, FOX_W), mp[1].reshape(np_, NSA_W), norm_fox[l], g_n, w_f, w_n,
                    _tile(np_, 512))
        xs = _merge(xs, msm[0].reshape(ns, FOX_W), msm[1].reshape(ns, NSA_W), norm_fox[l], g_n, w_f, w_n,
                    _tile(ns, 256))
        wqt = peer_wq[l].T.astype(BF16)
        sk = peer_subkeys[l].astype(BF16)
        u_bf = peer_u[l].astype(BF16)
        vt_bf = peer_v[l].T.astype(BF16)
        xp = _peer_channel_t(xp.T, norm_ffn[l], wqt, sk, u_bf, vt_bf, _tile(np_, 512)).T
        xs = _peer_channel_t(xs.T, norm_ffn[l], wqt, sk, u_bf, vt_bf, _tile(ns, 256)).T
        for i in range(4):
            outs[i].append(mp[2 + i])
            outs[4 + i].append(msm[2 + i])
    y_prompt = _rmsnorm(xp, norm_final, _tile(np_, 512)).reshape(b, t, d)
    y_sample = _rmsnorm(xs, norm_final, _tile(ns, 256)).reshape(bd, s, d)
    return (y_prompt, y_sample) + tuple(jnp.stack(o) for o in outs)
```

```python
import functools

import jax
import jax.numpy as jnp
import numpy as np
from jax import lax
from jax.experimental import pallas as pl
from jax.experimental.pallas import tpu as pltpu

HEAD_DIM = 64
FOX_HEADS = 8
NSA_HEADS = 8
NSA_KV = 2
NSA_HPG = NSA_HEADS // NSA_KV
FOX_W = FOX_HEADS * HEAD_DIM
NSA_W = NSA_HEADS * HEAD_DIM
NSA_KV_W = NSA_KV * HEAD_DIM
CMP_LEN = 32
CMP_STRIDE = 16
SEL_BLOCK = 64
N_SELECT = 16
WINDOW = 512
FORCED_SCORE = 1e4
ROPE_THETA = 10000.0
ATTN_SCALE = HEAD_DIM ** -0.5
PEER_KEYS = 128
PEER_HEADS = 8
PEER_QDIM = 256
PEER_TOPK = 16
RMS_EPS = 1e-6

LANES = 128
VMEM_LIMIT = 56 * 1024 * 1024
BF16 = jnp.bfloat16
F32 = jnp.float32
NEG_INF = float("-inf")


def _cparams(*sem):
    return pltpu.CompilerParams(dimension_semantics=sem, vmem_limit_bytes=VMEM_LIMIT)


def _gelu(x):
    return 0.5 * x * (1.0 + lax.erf(x * np.float32(np.sqrt(0.5))))


def _extract_topk(x, k):
    n_rows = x.shape[0]
    iota = lax.broadcasted_iota(jnp.int32, x.shape, 0).astype(F32)
    rank = jnp.full(x.shape, k, jnp.int32)
    vals, idxs = [], []
    for r in range(k):
        m = jnp.max(x, axis=0, keepdims=True)
        i = jnp.min(jnp.where(x == m, iota, float(n_rows)), axis=0, keepdims=True)
        hit = iota == i
        vals.append(m)
        idxs.append(i)
        rank = jnp.where(hit, r, rank)
        x = jnp.where(hit, NEG_INF, x)
    return vals, idxs, rank


_PEER_CAND = [(a, b) for a in range(PEER_TOPK) for b in range(PEER_TOPK) if (a + 1) * (b + 1) <= PEER_TOPK]


def _peer_route_kernel(xt_ref, g_ref, wqt_ref, sk_ref, ht_ref, r1_ref, cnt_ref, a_ref, b_ref):
    x = xt_ref[...]
    ms = jnp.mean(x * x, axis=0, keepdims=True)
    h = (x * lax.rsqrt(ms + RMS_EPS)) * g_ref[...]
    hb = h.astype(BF16)
    ht_ref[...] = hb
    q = jnp.dot(wqt_ref[...], hb, preferred_element_type=F32)
    half = PEER_QDIM // 2
    tn = x.shape[1]
    lane_w = min(LANES, tn)
    for hd, cb in [(hd, cb) for hd in range(PEER_HEADS) for cb in range(tn // lane_w)]:
        cols = slice(cb * lane_w, (cb + 1) * lane_w)
        q0 = q[hd * PEER_QDIM: hd * PEER_QDIM + half, cols].astype(BF16)
        q1 = q[hd * PEER_QDIM + half: (hd + 1) * PEER_QDIM, cols].astype(BF16)
        s0 = jnp.dot(sk_ref[0], q0, preferred_element_type=F32)
        s1 = jnp.dot(sk_ref[1], q1, preferred_element_type=F32)
        v0, i0, _ = _extract_topk(s0, PEER_TOPK)
        v1, _, rank1 = _extract_topk(s1, PEER_TOPK)
        cand = jnp.concatenate([v0[a] + v1[b] for a, b in _PEER_CAND], axis=0)
        fv, _, crank = _extract_topk(cand, PEER_TOPK)
        sel = crank < PEER_TOPK
        z = jnp.zeros_like(fv[0])
        for r in range(PEER_TOPK):
            z = z + jnp.exp(fv[r] - fv[0])
        inv_z = 1.0 / z
        row = 0
        iota = lax.broadcasted_iota(jnp.int32, s0.shape, 0).astype(F32)
        cnt_full = jnp.zeros(s0.shape, F32)
        for a in range(PEER_TOPK):
            nb = PEER_TOPK // (a + 1)
            cnt_a = jnp.sum(sel[row:row + nb].astype(F32), axis=0, keepdims=True)
            row += nb
            cnt_full = jnp.where(iota == i0[a], cnt_a, cnt_full)
        sl = slice(hd * PEER_KEYS, (hd + 1) * PEER_KEYS)
        r1_ref[sl, cols] = rank1.astype(F32).astype(BF16)
        cnt_ref[sl, cols] = cnt_full
        a_ref[sl, cols] = jnp.exp(s0 - v0[0])
        b_ref[sl, cols] = (jnp.exp(s1 - v1[0]) * inv_z).astype(BF16)


def _peer_dense_kernel(xt_ref, ht_ref, r1_ref, cnt_ref, a_ref, b_ref, u0_ref, un_ref, vt_ref, o_ref, *pre_refs,
                       rows_per_step):
    c = pl.program_id(1)
    pair = 2 * PEER_KEYS
    n_pairs = rows_per_step // 2

    @pl.when(c == 0)
    def _():
        o_ref[...] = xt_ref[...]
        for jp in range(n_pairs):
            pre_refs[jp][...] = jnp.dot(u0_ref[jp * pair:(jp + 1) * pair, :], ht_ref[...],
                                        preferred_element_type=F32)

    acc = None
    for jp in range(n_pairs):
        parts = []
        for j in range(2 * jp, 2 * jp + 2):
            act = _gelu(pre_refs[jp][(j - 2 * jp) * PEER_KEYS:(j - 2 * jp + 1) * PEER_KEYS, :])
            i1 = c * rows_per_step + j
            g = jnp.zeros(act.shape, BF16)
            for hd in range(PEER_HEADS):
                sl = slice(hd * PEER_KEYS, (hd + 1) * PEER_KEYS)
                cnt_row = cnt_ref[pl.ds(hd * PEER_KEYS + i1, 1), :].astype(BF16)
                a_row = a_ref[pl.ds(hd * PEER_KEYS + i1, 1), :].astype(BF16)
                g = g + jnp.where(r1_ref[sl, :] < cnt_row, b_ref[sl, :] * a_row, jnp.zeros_like(g))
            parts.append(g * act.astype(BF16))
        rows = slice(jp * pair, (jp + 1) * pair)
        pre_refs[jp][...] = jnp.dot(un_ref[rows, :], ht_ref[...], preferred_element_type=F32)
        part = jnp.dot(vt_ref[:, rows], jnp.concatenate(parts, axis=0), preferred_element_type=F32)
        acc = part if acc is None else acc + part
    o_ref[...] += acc


def _peer_channel_t(xt, g_ffn, wqt, sk, u_bf, vt_bf, tn, rows_per_step=8):
    d, n = xt.shape
    nt = n // tn
    hk = PEER_HEADS * PEER_KEYS
    tok = lambda i: (0, i)
    full = lambda i: (0, 0)
    ht, r1, cnt, a, b = pl.pallas_call(
        _peer_route_kernel,
        grid=(nt,),
        in_specs=[pl.BlockSpec((d, tn), tok), pl.BlockSpec((d, 1), full),
                  pl.BlockSpec(wqt.shape, full), pl.BlockSpec(sk.shape, lambda i: (0, 0, 0))],
        out_specs=[pl.BlockSpec((d, tn), tok)] + [pl.BlockSpec((hk, tn), tok)] * 4,
        out_shape=[jax.ShapeDtypeStruct((d, n), BF16)] + [jax.ShapeDtypeStruct((hk, n), dt) for dt in (BF16, F32, F32, BF16)],
        compiler_params=_cparams("parallel"),
    )(xt, g_ffn.reshape(d, 1), wqt, sk)
    ec = rows_per_step * PEER_KEYS
    n_chunks = u_bf.shape[0] // ec
    tok2 = lambda i, c: (0, i)
    return pl.pallas_call(
        functools.partial(_peer_dense_kernel, rows_per_step=rows_per_step),
        grid=(nt, n_chunks),
        in_specs=[pl.BlockSpec((d, tn), tok2), pl.BlockSpec((d, tn), tok2)]
                 + [pl.BlockSpec((hk, tn), tok2)] * 4
                 + [pl.BlockSpec((ec, d), lambda i, c: (0, 0)),
                    pl.BlockSpec((ec, d), lambda i, c: (jnp.minimum(c + 1, n_chunks - 1), 0)),
                    pl.BlockSpec((d, ec), lambda i, c: (0, c))],
        out_specs=pl.BlockSpec((d, tn), tok2),
        out_shape=jax.ShapeDtypeStruct((d, n), F32),
        scratch_shapes=[pltpu.VMEM((2 * PEER_KEYS, tn), F32)] * (rows_per_step // 2),
        compiler_params=_cparams("parallel", "arbitrary"),
    )(xt, ht, r1, cnt, a, b, u_bf, u_bf, vt_bf)


_C_FQ = 0
_C_FKV = _C_FQ + FOX_W
_C_MISC = _C_FKV + 2 * FOX_W
_C_NQ = _C_MISC + LANES
_C_NQR = _C_NQ + NSA_W
_C_NKV = _C_NQR + NSA_W
_C_NKR = _C_NKV + 6 * NSA_KV_W
_C_END = _C_NKR + 3 * NSA_KV_W
N_GATES = 3 * NSA_HEADS


def _rot_cols(w):
    d, c = w.shape
    half = HEAD_DIM // 2
    w4 = w.reshape(d, c // HEAD_DIM, 2, half)
    return jnp.stack([-w4[:, :, 1], w4[:, :, 0]], axis=2).reshape(d, c)


_NSA_PAIR_ORDER = [g * NSA_HPG + j for j in range(NSA_HPG) for g in range(NSA_KV)]


def _pair_perm(n_per_head):
    return np.concatenate([np.arange(h * n_per_head, (h + 1) * n_per_head) for h in _NSA_PAIR_ORDER])


def _prep_w_in(w_in, fox_fb):
    d = w_in.shape[0]
    o = 0
    fq = w_in[:, o:o + FOX_W]; o += FOX_W
    fkv = w_in[:, o:o + 2 * FOX_W]; o += 2 * FOX_W
    ff = w_in[:, o:o + FOX_HEADS]; o += FOX_HEADS
    nq = w_in[:, o:o + NSA_W]; o += NSA_W
    nkv = w_in[:, o:o + 6 * NSA_KV_W]; o += 6 * NSA_KV_W
    ng = w_in[:, o:o + N_GATES]
    misc = jnp.concatenate([ff, ng, jnp.zeros((d, LANES - FOX_HEADS - N_GATES), w_in.dtype)], axis=1)
    nq = nq[:, _pair_perm(HEAD_DIM)]
    nk = nkv.reshape(d, 3, 2, NSA_KV_W)[:, :, 0].reshape(d, 3 * NSA_KV_W)
    w_all = jnp.concatenate([fq, fkv, misc, nq, _rot_cols(nq), nkv, _rot_cols(nk)], axis=1).astype(BF16)
    fb = jnp.concatenate([fox_fb.astype(F32), jnp.zeros((LANES - FOX_HEADS,), F32)]).reshape(1, LANES)
    return w_all, fb


def _rope_tables(pos):
    half = HEAD_DIM // 2
    inv = ROPE_THETA ** (-jnp.arange(half, dtype=F32) / half)
    ang = pos.astype(F32)[:, None] * inv[None, :]
    reps = LANES // half
    return jnp.tile(jnp.cos(ang), (1, reps)), jnp.tile(jnp.sin(ang), (1, reps))


def _proj_kernel(x_ref, g_ref, w_ref, fb_ref, cos_ref, sin_ref,
                 fq_ref, fkv_ref, fkvb_ref, misc_ref, nq_ref, nkv_ref, nkvb_ref):
    x = x_ref[...]
    ms = jnp.mean(x * x, axis=-1, keepdims=True)
    hb = ((x * lax.rsqrt(ms + RMS_EPS)) * g_ref[...]).astype(BF16)

    def mm(c0, c1):
        return jnp.dot(hb, w_ref[:, c0:c1], preferred_element_type=F32)

    fq_ref[...] = (mm(_C_FQ, _C_FKV) * ATTN_SCALE).astype(BF16)
    fkv = mm(_C_FKV, _C_MISC)
    fkv_ref[...] = fkv
    fkvb_ref[...] = fkv.astype(BF16)
    z = mm(_C_MISC, _C_NQ) + fb_ref[...]
    lane = lax.broadcasted_iota(jnp.int32, z.shape, 1)
    log_sig = jnp.minimum(z, 0.0) - jnp.log1p(jnp.exp(-jnp.abs(z)))
    misc_ref[...] = jnp.where(lane < FOX_HEADS, log_sig, jax.nn.sigmoid(z))
    cos = cos_ref[...]
    sin = sin_ref[...]
    reps = NSA_W // LANES
    cos_q = jnp.concatenate([cos] * reps, axis=1)
    sin_q = jnp.concatenate([sin] * reps, axis=1)
    nq = mm(_C_NQ, _C_NQR) * cos_q + mm(_C_NQR, _C_NKV) * sin_q
    nq_ref[...] = (nq * ATTN_SCALE).astype(BF16)
    for s in range(6):
        c0 = _C_NKV + s * NSA_KV_W
        v = mm(c0, c0 + NSA_KV_W)
        if s % 2 == 0:
            r0 = _C_NKR + (s // 2) * NSA_KV_W
            v = v * cos + mm(r0, r0 + NSA_KV_W) * sin
        nkv_ref[:, s * NSA_KV_W:(s + 1) * NSA_KV_W] = v
        nkvb_ref[:, s * NSA_KV_W:(s + 1) * NSA_KV_W] = v.astype(BF16)


def _project(x2, g_mix, w_all, fb, cos, sin, tm):
    n, d = x2.shape
    tok = lambda i: (i, 0)
    full = lambda i: (0, 0)
    widths = [(FOX_W, BF16), (2 * FOX_W, F32), (2 * FOX_W, BF16), (LANES, F32), (NSA_W, BF16),
              (6 * NSA_KV_W, F32), (6 * NSA_KV_W, BF16)]
    return pl.pallas_call(
        _proj_kernel,
        grid=(n // tm,),
        in_specs=[pl.BlockSpec((tm, d), tok), pl.BlockSpec((1, d), full), pl.BlockSpec(w_all.shape, full),
                  pl.BlockSpec((1, LANES), full), pl.BlockSpec((tm, LANES), tok), pl.BlockSpec((tm, LANES), tok)],
        out_specs=[pl.BlockSpec((tm, w), tok) for w, _ in widths],
        out_shape=[jax.ShapeDtypeStruct((n, w), dt) for w, dt in widths],
        compiler_params=_cparams("parallel"),
    )(x2, g_mix.reshape(1, d), w_all, fb, cos, sin)


def _merge_kernel(x_ref, of_ref, on_ref, gf_ref, gn_ref, wf_ref, wn_ref, o_ref):
    def norm(v, g):
        v = v.astype(F32)
        ms = jnp.mean(v * v, axis=-1, keepdims=True)
        return ((v * lax.rsqrt(ms + RMS_EPS)) * g).astype(BF16)

    yf = norm(of_ref[...], gf_ref[...])
    yn = norm(on_ref[...], gn_ref[...])
    o_ref[...] = (x_ref[...] + jnp.dot(yf, wf_ref[...], preferred_element_type=F32)
                  + jnp.dot(yn, wn_ref[...], preferred_element_type=F32))


def _merge(x2, of, on, g_fox, g_nsa, w_f, w_n, tm):
    n, d = x2.shape
    tok = lambda i: (i, 0)
    full = lambda i: (0, 0)
    return pl.pallas_call(
        _merge_kernel,
        grid=(n // tm,),
        in_specs=[pl.BlockSpec((tm, d), tok), pl.BlockSpec((tm, FOX_W), tok), pl.BlockSpec((tm, NSA_W), tok),
                  pl.BlockSpec((1, FOX_W), full), pl.BlockSpec((1, NSA_W), full),
                  pl.BlockSpec((FOX_W, d), full), pl.BlockSpec((NSA_W, d), full)],
        out_specs=pl.BlockSpec((tm, d), tok),
        out_shape=jax.ShapeDtypeStruct((n, d), F32),
        compiler_params=_cparams("parallel"),
    )(x2, of, on, g_fox.reshape(1, -1), g_nsa.reshape(1, -1), w_f, w_n)


_NT = (((1,), (1,)), ((), ()))


def _flash_init(m_ref, l_ref, acc_ref):
    m_ref[...] = jnp.full(m_ref.shape, NEG_INF, F32)
    l_ref[...] = jnp.zeros(l_ref.shape, F32)
    acc_ref[...] = jnp.zeros(acc_ref.shape, F32)


def _flash_steps(qhs, load_k, load_v, bias_fn, lo, hi, tk, m_ref, l_ref, acc_ref):
    def body(kt, carry):
        ks = pl.multiple_of(kt * tk, tk)
        k = load_k(ks)
        v = load_v(ks)
        biases = bias_fn(ks)
        for n, qh in enumerate(qhs):
            s = lax.dot_general(qh, k, _NT, preferred_element_type=F32) + biases[n]
            m_prev = m_ref[n]
            m_new = jnp.maximum(m_prev, jnp.max(s, axis=1, keepdims=True))
            m_safe = jnp.where(m_new == NEG_INF, 0.0, m_new)
            p = jnp.exp(s - _lane_repeat(m_safe, tk))
            alpha = jnp.exp(m_prev - m_safe)
            l_ref[n] = alpha * l_ref[n] + jnp.sum(p, axis=1, keepdims=True)
            acc_ref[n] = alpha * acc_ref[n] + jnp.dot(p.astype(BF16), v, preferred_element_type=F32)
            m_ref[n] = m_new
        return carry

    lax.fori_loop(lo, hi, body, 0)


def _lane_repeat(x, width):
    return jnp.concatenate([x] * (width // LANES), axis=1)


def _flash_finish(l_ref, acc_ref):
    return acc_ref[...] / jnp.maximum(l_ref[...], 1e-30)


def _half_mask(shape, hh):
    lane = lax.broadcasted_iota(jnp.int32, shape, 1)
    return (lane >= HEAD_DIM) if hh else (lane < HEAD_DIM)


def _causal_bias(bias, qstart, ks):
    qpos = qstart + lax.broadcasted_iota(jnp.int32, bias.shape, 0)
    kpos = ks + lax.broadcasted_iota(jnp.int32, bias.shape, 1)
    return jnp.where(kpos <= qpos, bias, NEG_INF)


def _fox_prompt_kernel(q_ref, k_ref, v_ref, cq_ref, ck_ref, o_ref, m_ref, l_ref, acc_ref, *, tq):
    j = pl.program_id(1)
    qi = pl.program_id(2)
    qstart = qi * tq
    qp = q_ref[0]
    load_k = lambda ks: k_ref[0, pl.ds(ks, tq), :]
    load_v = lambda ks: v_ref[0, pl.ds(ks, tq), :]
    halves = [_half_mask(qp.shape, hh) for hh in range(2)]
    qhs = [jnp.where(half, qp, jnp.zeros_like(qp)) for half in halves]
    cqs = [_lane_repeat(jnp.broadcast_to(cq_ref[0, 0, :, hh:hh + 1], (tq, LANES)), tq) for hh in range(2)]
    bias = lambda ks: [cqs[hh] - ck_ref[0, pl.ds(2 * j + hh, 1), pl.ds(ks, tq)] for hh in range(2)]
    bias_diag = lambda ks: [_causal_bias(bb, qstart, ks) for bb in bias(ks)]
    _flash_init(m_ref, l_ref, acc_ref)
    _flash_steps(qhs, load_k, load_v, bias, 0, qi, tq, m_ref, l_ref, acc_ref)
    _flash_steps(qhs, load_k, load_v, bias_diag, qi, qi + 1, tq, m_ref, l_ref, acc_ref)
    o = _flash_finish(l_ref, acc_ref)
    o_ref[0] = jnp.where(halves[0], o[0], o[1])


def _fox_prompt(fq, fkvb, c, tq):
    b, t, _ = fq.shape
    n_pairs = FOX_W // LANES
    cq = c.reshape(b, t, n_pairs, 2).transpose(0, 2, 1, 3)
    ck = c.transpose(0, 2, 1)
    return pl.pallas_call(
        functools.partial(_fox_prompt_kernel, tq=tq),
        grid=(b, n_pairs, t // tq),
        in_specs=[pl.BlockSpec((1, tq, LANES), lambda bi, j, qi: (bi, qi, j)),
                  pl.BlockSpec((1, t, LANES), lambda bi, j, qi: (bi, 0, j)),
                  pl.BlockSpec((1, t, LANES), lambda bi, j, qi: (bi, 0, n_pairs + j)),
                  pl.BlockSpec((1, 1, tq, 2), lambda bi, j, qi: (bi, j, qi, 0)),
                  pl.BlockSpec((1, FOX_HEADS, t), lambda bi, j, qi: (bi, 0, 0))],
        out_specs=pl.BlockSpec((1, tq, LANES), lambda bi, j, qi: (bi, qi, j)),
        out_shape=jax.ShapeDtypeStruct((b, t, FOX_W), F32),
        scratch_shapes=[pltpu.VMEM((2, tq, LANES), F32), pltpu.VMEM((2, tq, LANES), F32), pltpu.VMEM((2, tq, LANES), F32)],
        compiler_params=_cparams("parallel", "parallel", "arbitrary"),
    )(fq, fkvb, fkvb, cq, ck)


def _round_up(x, m):
    return -(-x // m) * m


def _sel_cover_t(nc, nb, ncp):
    cs = np.arange(ncp)[None, :] * CMP_STRIDE
    bs = np.arange(nb)[:, None] * SEL_BLOCK
    cov = np.clip(np.minimum(cs + CMP_LEN, bs + SEL_BLOCK) - np.maximum(cs, bs), 0, None)
    cov = np.where(np.arange(ncp)[None, :] < nc, cov, 0)
    return jnp.asarray(cov.astype(np.float32) / CMP_LEN, BF16)


def _nsa_prompt_kernel(q_ref, kv_ref, kc_ref, vc_ref, misc_ref, covt_ref, o_ref,
                       qh_ref, oc_ref, bias_ref, m_ref, l_ref, acc_ref, *, tq, nc, nb, n_sel):
    qi = pl.program_id(1)
    qstart = qi * tq
    ncp = kc_ref.shape[1]
    n_pairs = NSA_W // LANES
    load = lambda slot: (lambda ks: kv_ref[0, pl.ds(ks, tq), slot * LANES:(slot + 1) * LANES])

    cidx = lax.broadcasted_iota(jnp.int32, (tq, ncp), 1)
    qpos_c = qstart + lax.broadcasted_iota(jnp.int32, (tq, ncp), 0)
    cmask = (cidx * CMP_STRIDE + (CMP_LEN - 1) <= qpos_c) & (cidx < nc)
    psum = [jnp.zeros((tq, ncp), F32) for _ in range(NSA_KV)]
    for j in range(n_pairs):
        qp = q_ref[0, :, j * LANES:(j + 1) * LANES]
        for g in range(NSA_KV):
            h = j * NSA_KV + g
            qh = jnp.where(_half_mask(qp.shape, g), qp, jnp.zeros_like(qp))
            qh_ref[h] = qh
            s = lax.dot_general(qh, kc_ref[0], _NT, preferred_element_type=F32)
            s = jnp.where(cmask, s, NEG_INF)
            m = jnp.max(s, axis=1, keepdims=True)
            m = jnp.where(m == NEG_INF, 0.0, m)
            e = jnp.exp(s - m)
            p = e / jnp.maximum(jnp.sum(e, axis=1, keepdims=True), 1e-30)
            psum[g] = psum[g] + p
            oc_ref[h] = jnp.dot(p.astype(BF16), vc_ref[0], preferred_element_type=F32)

    for g in range(NSA_KV):
        hi = psum[g].astype(BF16)
        lo = (psum[g] - hi.astype(F32)).astype(BF16)
        pb = (lax.dot_general(covt_ref[...], hi, _NT, preferred_element_type=F32)
              + lax.dot_general(covt_ref[...], lo, _NT, preferred_element_type=F32))
        jj = lax.broadcasted_iota(jnp.int32, (nb, tq), 0)
        cur = (qstart + lax.broadcasted_iota(jnp.int32, (nb, tq), 1)) // SEL_BLOCK
        causal = jj <= cur
        forced = (jj == 0) | (jj == cur) | (jj == cur - 1)
        score = jnp.where(forced, FORCED_SCORE, jnp.where(causal, pb, NEG_INF))
        _, _, rank = _extract_topk(score, n_sel)
        sel_t = jnp.where((rank < n_sel) & causal, 1.0, 0.0)
        sel_t = jnp.concatenate([sel_t, jnp.zeros((LANES - nb, tq), F32)], axis=0)
        sel = sel_t.T.astype(BF16)

        def fill(kt, carry):
            ks = pl.multiple_of(kt * tq, tq)
            blk = (ks + lax.broadcasted_iota(jnp.int32, (LANES, tq), 1)) // SEL_BLOCK
            expand = jnp.where(lax.broadcasted_iota(jnp.int32, (LANES, tq), 0) == blk, 1.0, 0.0).astype(BF16)
            hit = jnp.dot(sel, expand, preferred_element_type=F32)
            bias = jnp.where(hit > 0.5, 0.0, NEG_INF)
            bias_ref[g, :, pl.ds(ks, tq)] = _causal_bias(bias, qstart, ks)
            return carry

        lax.fori_loop(0, qi + 1, fill, 0)

    def win_bias(ks):
        qpos = qstart + lax.broadcasted_iota(jnp.int32, (tq, tq), 0)
        kpos = ks + lax.broadcasted_iota(jnp.int32, (tq, tq), 1)
        dist = qpos - kpos
        return jnp.where((dist >= 0) & (dist < WINDOW), 0.0, NEG_INF)

    win_lo = jnp.maximum(qi - (-(-WINDOW // tq)), 0)
    sel_bias = lambda ks: [bias_ref[g, :, pl.ds(ks, tq)] for g in range(NSA_KV)]
    win_bias2 = lambda ks: [win_bias(ks)] * NSA_KV
    for j in range(n_pairs):
        qhs = [qh_ref[j * NSA_KV + g] for g in range(NSA_KV)]
        _flash_init(m_ref, l_ref, acc_ref)
        _flash_steps(qhs, load(2), load(3), sel_bias, 0, qi + 1, tq, m_ref, l_ref, acc_ref)
        o_s = _flash_finish(l_ref, acc_ref)
        _flash_init(m_ref, l_ref, acc_ref)
        _flash_steps(qhs, load(4), load(5), win_bias2, win_lo, qi + 1, tq, m_ref, l_ref, acc_ref)
        o_w = _flash_finish(l_ref, acc_ref)
        outs = []
        for g in range(NSA_KV):
            col = FOX_HEADS + (g * NSA_HPG + j) * 3
            gates = [misc_ref[0, :, col + r:col + r + 1] for r in range(3)]
            outs.append(gates[0] * oc_ref[j * NSA_KV + g] + gates[1] * o_s[g] + gates[2] * o_w[g])
        o_ref[0, :, j * LANES:(j + 1) * LANES] = jnp.where(_half_mask(outs[0].shape, 0), outs[0], outs[1])


def _nsa_prompt(nq, nkvb, kc, vc, misc, tq):
    b, t, _ = nq.shape
    nc = kc.shape[1] - 1
    ncp = _round_up(nc + 1, LANES)
    nb = -(-t // SEL_BLOCK)
    assert nb <= LANES and t % tq == 0 and tq % SEL_BLOCK == 0
    kcp = _pad_rows(kc, ncp)
    vcp = _pad_rows(vc, ncp)
    covt = _sel_cover_t(nc, nb, ncp)
    kern = functools.partial(_nsa_prompt_kernel, tq=tq, nc=nc, nb=nb, n_sel=min(N_SELECT, nb))
    seq = lambda bi, qi: (bi, 0, 0)
    tile = lambda bi, qi: (bi, qi, 0)
    return pl.pallas_call(
        kern,
        grid=(b, t // tq),
        in_specs=[pl.BlockSpec((1, tq, NSA_W), tile), pl.BlockSpec((1, t, 6 * NSA_KV_W), seq),
                  pl.BlockSpec((1, ncp, LANES), seq), pl.BlockSpec((1, ncp, LANES), seq),
                  pl.BlockSpec((1, tq, LANES), tile), pl.BlockSpec((nb, ncp), lambda bi, qi: (0, 0))],
        out_specs=pl.BlockSpec((1, tq, NSA_W), tile),
        out_shape=jax.ShapeDtypeStruct((b, t, NSA_W), F32),
        scratch_shapes=[pltpu.VMEM((NSA_HEADS, tq, LANES), BF16), pltpu.VMEM((NSA_HEADS, tq, LANES), F32),
                        pltpu.VMEM((NSA_KV, tq, t), F32),
                        pltpu.VMEM((NSA_KV, tq, LANES), F32), pltpu.VMEM((NSA_KV, tq, LANES), F32),
                        pltpu.VMEM((NSA_KV, tq, LANES), F32)],
        compiler_params=_cparams("parallel", "arbitrary"),
    )(nq, nkvb, kcp, vcp, misc, covt)


def _tile(n, pref):
    t = min(pref, n)
    while n % t:
        t //= 2
    return t


def _mixer_prompt(x, g_mix, w_all, fb, cmp_weights):
    b, t, d = x.shape
    n = b * t
    cos, sin = _rope_tables(jnp.arange(t))
    cos = jnp.tile(cos, (b, 1))
    sin = jnp.tile(sin, (b, 1))
    fq, fkv, fkvb, misc, nq, nkv, nkvb = _project(x.reshape(n, d), g_mix, w_all, fb, cos, sin, _tile(n, 512))
    logf = misc[:, :FOX_HEADS].reshape(b, t, FOX_HEADS)
    c = jnp.cumsum(logf, axis=1)
    tq = _tile(t, 512)
    of = _fox_prompt(fq.reshape(b, t, FOX_W), fkvb.reshape(b, t, 2 * FOX_W), c, tq)
    rows = nkv.reshape(b, t, 6, NSA_KV, HEAD_DIM)
    page_rows = _tile(t, LANES)
    pages = jnp.arange(n // page_rows, dtype=jnp.int32).reshape(b, t // page_rows)
    kc, vc = _nsa_cmp_sample(0, pages, nkv.reshape(1, n // page_rows, page_rows, 6 * NSA_KV_W), *cmp_weights,
                             _tile(t // page_rows, 16), row_major=True)
    on = _nsa_prompt(nq.reshape(b, t, NSA_W), nkvb.reshape(b, t, 6 * NSA_KV_W), kc, vc, misc.reshape(b, t, LANES), tq)
    wb = min(WINDOW, t)
    return (of, on, fkv.reshape(b, t, 2, FOX_HEADS, HEAD_DIM), logf, rows[:, :, :4], rows[:, t - wb:, 4:])


def _split3(x):
    hi = x.astype(BF16)
    r = x - hi.astype(F32)
    mid = r.astype(BF16)
    return hi, mid, (r - mid.astype(F32)).astype(BF16)


def _split2(x):
    hi = x.astype(BF16)
    return hi, (x - hi.astype(F32)).astype(BF16)


def _softmax_update(s_list, v_list, m_ref, l_ref, acc_ref, v_transposed=False):
    pv = (lambda p, v: lax.dot_general(p, v, _NT, preferred_element_type=F32)) if v_transposed else (
        lambda p, v: jnp.dot(p, v, preferred_element_type=F32))
    m_prev = m_ref[...]
    m_new = m_prev
    for s in s_list:
        m_new = jnp.maximum(m_new, jnp.max(s, axis=1, keepdims=True))
    m_safe = jnp.where(m_new == NEG_INF, 0.0, m_new)
    alpha = jnp.exp(m_prev - m_safe)
    l_new = alpha * l_ref[...]
    acc = alpha * acc_ref[...]
    for s, v_bf in zip(s_list, v_list):
        p = jnp.exp(s - m_safe)
        l_new = l_new + jnp.sum(p, axis=1, keepdims=True)
        acc = acc + pv(p.astype(BF16), v_bf)
    l_ref[...] = l_new
    acc_ref[...] = acc
    m_ref[...] = m_new


def _fox_sample_kernel(pt_ref, q_ref, cnew_ref, knew_ref, vnew_ref, bnew_ref, scan_ref, *rest, n_pg, s_tok):
    kv_refs, lf_refs = rest[:n_pg], rest[n_pg:2 * n_pg]
    o_ref, m_ref, l_ref, acc_ref, carry_ref = rest[2 * n_pg:]
    st = pl.program_id(1)

    @pl.when(st == 0)
    def _():
        _flash_init(m_ref, l_ref, acc_ref)
        carry_ref[...] = jnp.zeros(carry_ref.shape, F32)

    q = q_ref[0]
    cnew = cnew_ref[0]
    page_rows = kv_refs[0].shape[2]
    carry = carry_ref[...]
    s_list, v_list = [], []
    for i in reversed(range(n_pg)):
        x = lf_refs[i][...]
        sc = sum(jnp.dot(p, scan_ref[...], preferred_element_type=F32) for p in _split3(x))
        d = sc[:, :page_rows] + carry
        carry = carry + sc[:, page_rows:]
        dd = jnp.concatenate([jnp.broadcast_to(d[h:h + 1], (s_tok, page_rows)) for h in range(FOX_HEADS)], axis=0)
        s_list.append(jnp.dot(q, kv_refs[i][0].astype(BF16), preferred_element_type=F32) + (cnew + dd))
        v_list.append(kv_refs[i][1].astype(BF16))
    carry_ref[...] = carry
    _softmax_update(s_list, v_list, m_ref, l_ref, acc_ref, v_transposed=True)

    @pl.when(st == pl.num_programs(1) - 1)
    def _():
        s = lax.dot_general(q, knew_ref[0], _NT, preferred_element_type=F32) + bnew_ref[0]
        _softmax_update([s], [vnew_ref[0]], m_ref, l_ref, acc_ref)
        o = _flash_finish(l_ref, acc_ref)
        lane_head = lax.broadcasted_iota(jnp.int32, (s_tok, FOX_W), 1) // HEAD_DIM
        out = jnp.zeros((s_tok, FOX_W), F32)
        for h in range(FOX_HEADS):
            out = jnp.where(lane_head == h, o[h * s_tok:(h + 1) * s_tok], out)
        o_ref[0] = out


def _fox_sample(layer, page_table, q_rows, cnew_b, knew, vnew, bnew, cache_kv, logf_t, n_pg):
    bd, n_pages = page_table.shape
    page_rows = cache_kv.shape[4]
    s_tok = q_rows.shape[1] // FOX_HEADS
    rows = q_rows.shape[1]
    kidx = np.arange(page_rows)
    scan = np.concatenate([(kidx[:, None] > kidx[None, :]), np.ones((page_rows, page_rows), bool)], axis=1)
    scan = jnp.asarray(scan, BF16)
    n_st = n_pages // n_pg
    seq = lambda b, st, pt: (b, 0, 0)

    def page_map(i, nd):
        return lambda b, st, pt: (layer, pt[b, n_pages - (st + 1) * n_pg + i]) + (0,) * nd

    grid_spec = pltpu.PrefetchScalarGridSpec(
        num_scalar_prefetch=1,
        grid=(bd, n_st),
        in_specs=[pl.BlockSpec((1, rows, FOX_W), seq), pl.BlockSpec((1, rows, page_rows), seq),
                  pl.BlockSpec((1, page_rows, FOX_W), seq), pl.BlockSpec((1, page_rows, FOX_W), seq),
                  pl.BlockSpec((1, rows, page_rows), seq),
                  pl.BlockSpec(scan.shape, lambda b, st, pt: (0, 0))]
                 + [pl.BlockSpec((None, None, 2, FOX_W, page_rows), page_map(i, 3)) for i in range(n_pg)]
                 + [pl.BlockSpec((None, None, FOX_HEADS, page_rows), page_map(i, 2)) for i in range(n_pg)],
        out_specs=pl.BlockSpec((1, s_tok, FOX_W), seq),
        scratch_shapes=[pltpu.VMEM((rows, 1), F32), pltpu.VMEM((rows, 1), F32), pltpu.VMEM((rows, FOX_W), F32),
                        pltpu.VMEM((FOX_HEADS, page_rows), F32)])
    return pl.pallas_call(
        functools.partial(_fox_sample_kernel, n_pg=n_pg, s_tok=s_tok),
        grid_spec=grid_spec,
        out_shape=jax.ShapeDtypeStruct((bd, s_tok, FOX_W), F32),
        compiler_params=_cparams("parallel", "arbitrary"),
    )(page_table, q_rows, cnew_b, knew, vnew, bnew, scan, *([cache_kv] * n_pg), *([logf_t] * n_pg))


def _nsa_cmp_sample_kernel(pt_ref, pe_ref, w1_ref, w2_ref, *rest, n_pg, row_major):
    pages = rest[:n_pg]
    kc_ref, vc_ref, carry_ref, xs_ref = rest[n_pg:]
    st = pl.program_id(1)

    @pl.when(st == 0)
    def _():
        carry_ref[...] = jnp.zeros(carry_ref.shape, F32)

    page_rows = pages[0].shape[0] if row_major else pages[0].shape[2]
    m = n_pg * page_rows // CMP_STRIDE
    last = lax.broadcasted_iota(jnp.int32, (m, NSA_KV_W), 0) == m - 1
    for slot, out_ref in ((0, kc_ref), (1, vc_ref)):
        for i, pg in enumerate(pages):
            xs_ref[i * page_rows:(i + 1) * page_rows, :] = (
                pg[:, slot * NSA_KV_W:(slot + 1) * NSA_KV_W] if row_major else pg[slot].T)
        a = jnp.zeros((m, NSA_KV_W), F32)
        b = jnp.zeros((m, NSA_KV_W), F32)
        for r in range(CMP_STRIDE):
            x = xs_ref[pl.ds(r, m, stride=CMP_STRIDE), :]
            xa = (x + pe_ref[slot, r:r + 1, :]).astype(BF16)
            xb = (x + pe_ref[slot, CMP_STRIDE + r:CMP_STRIDE + r + 1, :]).astype(BF16)
            a = a + jnp.dot(xa, w1_ref[slot, r], preferred_element_type=F32)
            b = b + jnp.dot(xb, w1_ref[slot, CMP_STRIDE + r], preferred_element_type=F32)
        b_next = jnp.where(last, carry_ref[slot, 0:1, :], pltpu.roll(b, m - 1, axis=0))
        carry_ref[slot] = b
        h = _gelu(a + b_next).astype(BF16)
        out_ref[0] = jnp.dot(h, w2_ref[slot], preferred_element_type=F32).astype(BF16)


def _nsa_cmp_sample(layer, page_table, cache_nsa, pe2, w1_bd, w2_bd, n_pg, row_major=False):
    bd, n_pages = page_table.shape
    page_rows = cache_nsa.shape[2] if row_major else cache_nsa.shape[4]
    cpp = page_rows // CMP_STRIDE
    m = n_pg * cpp
    n_st = n_pages // n_pg
    const = lambda nd: (lambda b, st, pt: (0,) * nd)

    def page_map(i):
        return lambda b, st, pt: (layer, pt[b, n_pages - (st + 1) * n_pg + i]) + (0,) * (2 if row_major else 3)

    page_block = (None, None, page_rows, 2 * NSA_KV_W) if row_major else (None, None, 2) + cache_nsa.shape[3:]
    out_spec = pl.BlockSpec((1, m, NSA_KV_W), lambda b, st, pt: (b, n_st - 1 - st, 0))
    grid_spec = pltpu.PrefetchScalarGridSpec(
        num_scalar_prefetch=1,
        grid=(bd, n_st),
        in_specs=[pl.BlockSpec(pe2.shape, const(3)), pl.BlockSpec(w1_bd.shape, const(4)),
                  pl.BlockSpec(w2_bd.shape, const(3))]
                 + [pl.BlockSpec(page_block, page_map(i)) for i in range(n_pg)],
        out_specs=[out_spec, out_spec],
        scratch_shapes=[pltpu.VMEM((2, m, NSA_KV_W), F32), pltpu.VMEM((n_pg * page_rows, NSA_KV_W), F32)])
    shape = jax.ShapeDtypeStruct((bd, n_pages * cpp, NSA_KV_W), BF16)
    return pl.pallas_call(
        functools.partial(_nsa_cmp_sample_kernel, n_pg=n_pg, row_major=row_major),
        grid_spec=grid_spec,
        out_shape=[shape, shape],
        compiler_params=_cparams("parallel", "arbitrary"),
    )(page_table, pe2, w1_bd, w2_bd, *([cache_nsa] * n_pg))


def _nsa_sample_kernel(pt_ref, q_ref, kc_ref, vc_ref, gate_ref, covt_ref, gsum_ref, win_ref, new_ref, *rest,
                       n_pg, s_tok, nc, nb, n_sel, past):
    pages = rest[:n_pg]
    o_ref, bias_ref, oc_ref, m_ref, l_ref, acc_ref = rest[n_pg:]
    st = pl.program_id(1)
    q = q_ref[0]
    rows = q.shape[0]
    page_rows = pages[0].shape[2]

    @pl.when(st == 0)
    def _():
        ncp = kc_ref.shape[1]
        nbp = covt_ref.shape[0]
        s = lax.dot_general(q, kc_ref[0], _NT, preferred_element_type=F32)
        cidx = lax.broadcasted_iota(jnp.int32, (rows, ncp), 1)
        qpos = past + lax.broadcasted_iota(jnp.int32, (rows, ncp), 0) % s_tok
        s = jnp.where((cidx * CMP_STRIDE + (CMP_LEN - 1) <= qpos) & (cidx < nc), s, NEG_INF)
        m = jnp.max(s, axis=1, keepdims=True)
        m = jnp.where(m == NEG_INF, 0.0, m)
        e = jnp.exp(s - m)
        p = e / jnp.maximum(jnp.sum(e, axis=1, keepdims=True), 1e-30)
        oc_ref[...] = jnp.dot(p.astype(BF16), vc_ref[0], preferred_element_type=F32)
        psum = sum(jnp.dot(gsum_ref[...], x, preferred_element_type=F32) for x in _split2(p))
        pb = sum(lax.dot_general(covt_ref[...], x, _NT, preferred_element_type=F32) for x in _split2(psum))
        jj = lax.broadcasted_iota(jnp.int32, (nbp, rows), 0)
        cur = (past + lax.broadcasted_iota(jnp.int32, (nbp, rows), 1) % s_tok) // SEL_BLOCK
        causal = (jj <= cur) & (jj < nb)
        forced = (jj == 0) | (jj == cur) | (jj == cur - 1)
        score = jnp.where(forced, FORCED_SCORE, jnp.where(causal, pb, NEG_INF))
        _, _, rank = _extract_topk(score, n_sel)
        sel = jnp.where((rank < n_sel) & causal, 1.0, 0.0).T.astype(BF16)
        chunk = min(past, 16 * page_rows)

        def fill(c, carry):
            ks = pl.multiple_of(c * chunk, chunk)
            blk = (ks + lax.broadcasted_iota(jnp.int32, (nbp, chunk), 1)) // SEL_BLOCK
            expand = jnp.where(lax.broadcasted_iota(jnp.int32, (nbp, chunk), 0) == blk, 1.0, 0.0).astype(BF16)
            hit = jnp.dot(sel, expand, preferred_element_type=F32)
            bias_ref[:, pl.ds(ks, chunk)] = jnp.where(hit > 0.5, 0.0, NEG_INF)
            return carry

        lax.fori_loop(0, bias_ref.shape[1] // chunk, fill, 0)
        _flash_init(m_ref, l_ref, acc_ref)

    s_list, v_list = [], []
    for i in range(n_pg):
        ks = pl.multiple_of((st * n_pg + i) * page_rows, page_rows)
        s = jnp.dot(q, pages[i][0].astype(BF16), preferred_element_type=F32)
        s_list.append(s + bias_ref[:, pl.ds(ks, page_rows)])
        v_list.append(pages[i][1].astype(BF16))
    _softmax_update(s_list, v_list, m_ref, l_ref, acc_ref, v_transposed=True)

    @pl.when(st == pl.num_programs(1) - 1)
    def _():
        tok = lax.broadcasted_iota(jnp.int32, (rows, page_rows), 0) % s_tok
        col = lax.broadcasted_iota(jnp.int32, (rows, page_rows), 1)
        bnew = jnp.where((col <= tok) & (col < s_tok), 0.0, NEG_INF)
        s = lax.dot_general(q, new_ref[0, 0], _NT, preferred_element_type=F32) + bnew
        _softmax_update([s], [new_ref[0, 1]], m_ref, l_ref, acc_ref)
        o_s = _flash_finish(l_ref, acc_ref)
        wb = win_ref.shape[2]
        wtok = lax.broadcasted_iota(jnp.int32, (rows, wb), 0) % s_tok
        wi = lax.broadcasted_iota(jnp.int32, (rows, wb), 1)
        dist = wb + wtok - wi
        ok = (past - wb + wi >= 0) & (dist >= 0) & (dist < WINDOW)
        s1 = jnp.dot(q, win_ref[0].astype(BF16), preferred_element_type=F32)
        s1 = jnp.where(ok, s1, NEG_INF)
        s2 = lax.dot_general(q, new_ref[0, 2], _NT, preferred_element_type=F32) + bnew
        m = jnp.maximum(jnp.max(s1, axis=1, keepdims=True), jnp.max(s2, axis=1, keepdims=True))
        e1 = jnp.exp(s1 - m)
        e2 = jnp.exp(s2 - m)
        den = jnp.sum(e1, axis=1, keepdims=True) + jnp.sum(e2, axis=1, keepdims=True)
        o_w = (lax.dot_general(e1.astype(BF16), win_ref[1].astype(BF16), _NT, preferred_element_type=F32)
               + jnp.dot(e2.astype(BF16), new_ref[0, 3], preferred_element_type=F32)) / den
        o = gate_ref[0, 0] * oc_ref[...] + gate_ref[0, 1] * o_s + gate_ref[0, 2] * o_w
        for j in range(NSA_W // LANES):
            lo = o[(2 * j) * s_tok:(2 * j + 1) * s_tok]
            hi = o[(2 * j + 1) * s_tok:(2 * j + 2) * s_tok]
            o_ref[0, :, j * LANES:(j + 1) * LANES] = jnp.where(_half_mask(lo.shape, 0), lo, hi)


def _nsa_sample(layer, page_table, q_rows, kc, vc, gates, win_state, new_kv, cache_nsa, s_tok, n_pg):
    bd, n_pages = page_table.shape
    page_rows = cache_nsa.shape[4]
    past = n_pages * page_rows
    rows = q_rows.shape[1]
    ncp = kc.shape[1]
    nc = ncp - 1
    nb = -(-(past + s_tok) // SEL_BLOCK)
    nbp = _round_up(nb, LANES)
    assert past % SEL_BLOCK == 0 and s_tok <= SEL_BLOCK and s_tok < CMP_STRIDE
    covt = _sel_cover_t(nc, nbp, ncp)
    r = np.arange(rows)
    same = (r[:, None] % s_tok == r[None, :] % s_tok) & ((r[:, None] // s_tok) % NSA_KV == (r[None, :] // s_tok) % NSA_KV)
    live = r < NSA_HEADS * s_tok
    gsum = jnp.asarray(same & live[:, None] & live[None, :], BF16)
    wb = win_state.shape[4]
    n_st = n_pages // n_pg
    seq = lambda b, st, pt: (b, 0, 0)
    const2 = lambda b, st, pt: (0, 0)
    grid_spec = pltpu.PrefetchScalarGridSpec(
        num_scalar_prefetch=1,
        grid=(bd, n_st),
        in_specs=[pl.BlockSpec((1, rows, LANES), seq), pl.BlockSpec((1, ncp, LANES), seq),
                  pl.BlockSpec((1, ncp, LANES), seq), pl.BlockSpec((1, 3, rows, LANES), lambda b, st, pt: (b, 0, 0, 0)),
                  pl.BlockSpec(covt.shape, const2), pl.BlockSpec(gsum.shape, const2),
                  pl.BlockSpec((None, None, 2, NSA_KV_W, wb), lambda b, st, pt: (layer, b, 0, 0, 0)),
                  pl.BlockSpec((1, 4, page_rows, LANES), lambda b, st, pt: (b, 0, 0, 0))]
                 + [pl.BlockSpec((None, None, 2) + cache_nsa.shape[3:],
                                 (lambda i: lambda b, st, pt: (layer, pt[b, st * n_pg + i], 1, 0, 0))(i))
                    for i in range(n_pg)],
        out_specs=pl.BlockSpec((1, s_tok, NSA_W), seq),
        scratch_shapes=[pltpu.VMEM((rows, past), F32), pltpu.VMEM((rows, LANES), F32),
                        pltpu.VMEM((rows, 1), F32), pltpu.VMEM((rows, 1), F32), pltpu.VMEM((rows, LANES), F32)])
    kern = functools.partial(_nsa_sample_kernel, n_pg=n_pg, s_tok=s_tok, nc=nc, nb=nb, n_sel=min(N_SELECT, nb),
                             past=past)
    return pl.pallas_call(
        kern,
        grid_spec=grid_spec,
        out_shape=jax.ShapeDtypeStruct((bd, s_tok, NSA_W), F32),
        compiler_params=_cparams("parallel", "arbitrary"),
    )(page_table, q_rows, kc, vc, gates, covt, gsum, win_state, new_kv, *([cache_nsa] * n_pg))


def _block_diag_groups(w):
    eye = jnp.eye(NSA_KV, dtype=w.dtype)
    out = jnp.einsum('gh,...de->...gdhe', eye, w)
    return out.reshape(w.shape[:-2] + (NSA_KV_W, NSA_KV_W))


def _pad_rows(x, rows, value=0.0):
    pad = [(0, 0)] * x.ndim
    pad[-2] = (0, rows - x.shape[-2])
    return jnp.pad(x, pad, constant_values=value)


def _cmp_weights(cmp_pe, cmp_w1, cmp_w2):
    pe2 = jnp.tile(cmp_pe, (1, 1, NSA_KV))
    w1_bd = _block_diag_groups(cmp_w1.reshape(2, CMP_LEN, HEAD_DIM, HEAD_DIM)).astype(BF16)
    return pe2, w1_bd, _block_diag_groups(cmp_w2).astype(BF16)


def _mixer_sample(x, layer, page_table, cache_fox_kv, logf_t, cache_nsa_kv, state_win, state_win_rows, g_mix, w_all,
                  fb, cmp_weights):
    bd, s, d = x.shape
    n_pages = page_table.shape[1]
    page_rows = cache_fox_kv.shape[4]
    past = n_pages * page_rows
    n = bd * s
    pos = past + jnp.arange(s)
    cos, sin = _rope_tables(pos)
    cos = jnp.tile(cos, (bd, 1))
    sin = jnp.tile(sin, (bd, 1))
    fq, fkv, fkvb, misc, nq, nkv, nkvb = _project(x.reshape(n, d), g_mix, w_all, fb, cos, sin, _tile(n, 256))
    logf = misc[:, :FOX_HEADS].reshape(bd, s, FOX_HEADS)
    cnew = jnp.cumsum(logf, axis=1).transpose(0, 2, 1)
    rows_f = FOX_HEADS * s
    cnew_b = jnp.broadcast_to(cnew.reshape(bd, rows_f, 1), (bd, rows_f, page_rows))
    causal = jnp.arange(s)[None, :] <= jnp.arange(s)[:, None]
    bnew = jnp.where(causal, cnew[:, :, :, None] - cnew[:, :, None, :], NEG_INF).reshape(bd, rows_f, s)
    bnew = jnp.pad(bnew, ((0, 0), (0, 0), (0, page_rows - s)), constant_values=NEG_INF)
    head_of_lane = jnp.arange(FOX_W) // HEAD_DIM
    fq3 = fq.reshape(bd, 1, s, FOX_W)
    q_rows = jnp.where(head_of_lane[None, None, None, :] == jnp.arange(FOX_HEADS)[None, :, None, None], fq3,
                       jnp.zeros_like(fq3)).reshape(bd, rows_f, FOX_W)
    fkvb3 = fkvb.reshape(bd, s, 2 * FOX_W)
    knew = _pad_rows(fkvb3[:, :, :FOX_W], page_rows)
    vnew = _pad_rows(fkvb3[:, :, FOX_W:], page_rows)
    of = _fox_sample(layer, page_table, q_rows, cnew_b, knew, vnew, bnew, cache_fox_kv, logf_t, _tile(n_pages, 16))
    n_pg = _tile(n_pages, 16)
    kc, vc = _nsa_cmp_sample(layer, page_table, cache_nsa_kv, *cmp_weights, n_pg)
    rows_n = _round_up(NSA_HEADS * s, LANES)
    nq4 = nq.reshape(bd, s, NSA_W // LANES, 1, LANES).transpose(0, 2, 3, 1, 4)
    half_of_lane = (jnp.arange(LANES) // HEAD_DIM)[None, None, None, None, :]
    qn = jnp.where(half_of_lane == jnp.arange(NSA_KV)[None, None, :, None, None], nq4, jnp.zeros_like(nq4))
    qn = _pad_rows(qn.reshape(bd, NSA_HEADS * s, LANES), rows_n)
    gate = misc[:, FOX_HEADS:FOX_HEADS + N_GATES].reshape(bd, s, NSA_KV, NSA_HPG, 3)
    gate = gate.transpose(0, 4, 3, 2, 1).reshape(bd, 3, NSA_HEADS * s, 1)
    gates = jnp.broadcast_to(_pad_rows(gate, rows_n), (bd, 3, rows_n, LANES))
    nkvb3 = nkvb.reshape(bd, s, 6, NSA_KV_W)
    new_kv = _pad_rows(nkvb3[:, :, 2:6].transpose(0, 2, 1, 3), page_rows)
    on = _nsa_sample(layer, page_table, qn, kc, vc, gates, state_win, new_kv, cache_nsa_kv, s, n_pg)
    rows = nkv.reshape(bd, s, 6, NSA_KV, HEAD_DIM)
    win_all = jnp.concatenate([state_win_rows, rows[:, :, 4:]], axis=1)
    return of, on, fkv.reshape(bd, s, 2, FOX_HEADS, HEAD_DIM), logf, rows[:, :, :4], win_all[:, s:]


def _rmsnorm_kernel(x_ref, g_ref, o_ref):
    x = x_ref[...]
    ms = jnp.mean(x * x, axis=-1, keepdims=True)
    o_ref[...] = (x * lax.rsqrt(ms + RMS_EPS)) * g_ref[...]


def _rmsnorm(x2, g, tm):
    n, d = x2.shape
    return pl.pallas_call(
        _rmsnorm_kernel,
        grid=(n // tm,),
        in_specs=[pl.BlockSpec((tm, d), lambda i: (i, 0)), pl.BlockSpec((1, d), lambda i: (0, 0))],
        out_specs=pl.BlockSpec((tm, d), lambda i: (i, 0)),
        out_shape=jax.ShapeDtypeStruct((n, d), F32),
        compiler_params=_cparams("parallel"),
    )(x2, g.reshape(1, d))


def kernel(x_prompt, x_sample, cache_fox_kv, cache_fox_logf, cache_nsa_kv, state_win_kv, page_table,
           norm_mix, w_in, fox_fb, cmp_pe, cmp_w1, cmp_w2, norm_fox, norm_nsa, w_out,
           norm_ffn, peer_wq, peer_subkeys, peer_u, peer_v, norm_final):
    depth = w_in.shape[0]
    b, t, d = x_prompt.shape
    bd, s, _ = x_sample.shape
    n_seq, n_pages = page_table.shape
    past = n_pages * cache_fox_kv.shape[2]
    np_, ns = b * t, bd * s
    xp = x_prompt.reshape(np_, d)
    xs = x_sample.reshape(ns, d)
    perm = _pair_perm(HEAD_DIM)
    outs = [[] for _ in range(8)]
    pool, page_rows = cache_fox_kv.shape[1:3]
    row_minor = (0, 1, 3, 4, 5, 2)
    fox_kv2 = cache_fox_kv.transpose(row_minor).reshape(depth, pool, 2, FOX_W, page_rows)
    logf_t = cache_fox_logf.transpose(0, 1, 3, 2)
    nsa_kv2 = cache_nsa_kv.transpose(row_minor).reshape(depth, pool, 4, NSA_KV_W, page_rows)
    win2 = state_win_kv.transpose(row_minor).reshape(depth, bd, 2, NSA_KV_W, state_win_kv.shape[2])
    for l in range(depth):
        w_all, fb = _prep_w_in(w_in[l], fox_fb[l])
        cmp_weights = _cmp_weights(cmp_pe[l], cmp_w1[l], cmp_w2[l])
        mp = _mixer_prompt(xp.reshape(b, t, d), norm_mix[l], w_all, fb, cmp_weights)
        msm = _mixer_sample(xs.reshape(bd, s, d), l, page_table, fox_kv2, logf_t, nsa_kv2, win2, state_win_kv[l],
                            norm_mix[l], w_all, fb, cmp_weights)
        w_f = w_out[l, :FOX_W].astype(BF16)
        w_n = w_out[l, FOX_W:][perm].astype(BF16)
        g_n = norm_nsa[l][perm]
        xp = _merge(xp, mp[0].reshape(np_, FOX_W), mp[1].reshape(np_, NSA_W), norm_fox[l], g_n, w_f, w_n,
                    _tile(np_, 512))
        xs = _merge(xs, msm[0].reshape(ns, FOX_W), msm[1].reshape(ns, NSA_W), norm_fox[l], g_n, w_f, w_n,
                    _tile(ns, 256))
        wqt = peer_wq[l].T.astype(BF16)
        sk = peer_subkeys[l].astype(BF16)
        u_bf = peer_u[l].astype(BF16)
        vt_bf = peer_v[l].T.astype(BF16)
        xp = _peer_channel_t(xp.T, norm_ffn[l], wqt, sk, u_bf, vt_bf, _tile(np_, 512)).T
        xs = _peer_channel_t(xs.T, norm_ffn[l], wqt, sk, u_bf, vt_bf, _tile(ns, 256)).T
        for i in range(4):
            outs[i].append(mp[2 + i])
            outs[4 + i].append(msm[2 + i])
    y_prompt = _rmsnorm(xp, norm_final, _tile(np_, 512)).reshape(b, t, d)
    y_sample = _rmsnorm(xs, norm_final, _tile(ns, 256)).reshape(bd, s, d)
    return (y_prompt, y_sample) + tuple(jnp.stack(o) for o in outs)
```

```python
import functools

import jax
import jax.numpy as jnp
import numpy as np
from jax import lax
from jax.experimental import pallas as pl
from jax.experimental.pallas import tpu as pltpu

HEAD_DIM = 64
FOX_HEADS = 8
NSA_HEADS = 8
NSA_KV = 2
NSA_HPG = NSA_HEADS // NSA_KV
FOX_W = FOX_HEADS * HEAD_DIM
NSA_W = NSA_HEADS * HEAD_DIM
NSA_KV_W = NSA_KV * HEAD_DIM
CMP_LEN = 32
CMP_STRIDE = 16
SEL_BLOCK = 64
N_SELECT = 16
WINDOW = 512
FORCED_SCORE = 1e4
ROPE_THETA = 10000.0
ATTN_SCALE = HEAD_DIM ** -0.5
PEER_KEYS = 128
PEER_HEADS = 8
PEER_QDIM = 256
PEER_TOPK = 16
RMS_EPS = 1e-6

LANES = 128
VMEM_LIMIT = 56 * 1024 * 1024
BF16 = jnp.bfloat16
F32 = jnp.float32
NEG_INF = float("-inf")


def _cparams(*sem):
    return pltpu.CompilerParams(dimension_semantics=sem, vmem_limit_bytes=VMEM_LIMIT)


def _gelu(x):
    return 0.5 * x * (1.0 + lax.erf(x * np.float32(np.sqrt(0.5))))


def _extract_topk(x, k):
    n_rows = x.shape[0]
    iota = lax.broadcasted_iota(jnp.int32, x.shape, 0).astype(F32)
    rank = jnp.full(x.shape, k, jnp.int32)
    vals, idxs = [], []
    for r in range(k):
        m = jnp.max(x, axis=0, keepdims=True)
        i = jnp.min(jnp.where(x == m, iota, float(n_rows)), axis=0, keepdims=True)
        hit = iota == i
        vals.append(m)
        idxs.append(i)
        rank = jnp.where(hit, r, rank)
        x = jnp.where(hit, NEG_INF, x)
    return vals, idxs, rank


_PEER_CAND = [(a, b) for a in range(PEER_TOPK) for b in range(PEER_TOPK) if (a + 1) * (b + 1) <= PEER_TOPK]


def _peer_route_kernel(xt_ref, g_ref, wqt_ref, sk_ref, ht_ref, r1_ref, cnt_ref, a_ref, b_ref):
    x = xt_ref[...]
    ms = jnp.mean(x * x, axis=0, keepdims=True)
    h = (x * lax.rsqrt(ms + RMS_EPS)) * g_ref[...]
    hb = h.astype(BF16)
    ht_ref[...] = hb
    q = jnp.dot(wqt_ref[...], hb, preferred_element_type=F32)
    half = PEER_QDIM // 2
    tn = x.shape[1]
    lane_w = min(LANES, tn)
    for hd, cb in [(hd, cb) for hd in range(PEER_HEADS) for cb in range(tn // lane_w)]:
        cols = slice(cb * lane_w, (cb + 1) * lane_w)
        q0 = q[hd * PEER_QDIM: hd * PEER_QDIM + half, cols].astype(BF16)
        q1 = q[hd * PEER_QDIM + half: (hd + 1) * PEER_QDIM, cols].astype(BF16)
        s0 = jnp.dot(sk_ref[0], q0, preferred_element_type=F32)
        s1 = jnp.dot(sk_ref[1], q1, preferred_element_type=F32)
        v0, i0, _ = _extract_topk(s0, PEER_TOPK)
        v1, _, rank1 = _extract_topk(s1, PEER_TOPK)
        cand = jnp.concatenate([v0[a] + v1[b] for a, b in _PEER_CAND], axis=0)
        fv, _, crank = _extract_topk(cand, PEER_TOPK)
        sel = crank < PEER_TOPK
        z = jnp.zeros_like(fv[0])
        for r in range(PEER_TOPK):
            z = z + jnp.exp(fv[r] - fv[0])
        inv_z = 1.0 / z
        row = 0
        iota = lax.broadcasted_iota(jnp.int32, s0.shape, 0).astype(F32)
        cnt_full = jnp.zeros(s0.shape, F32)
        for a in range(PEER_TOPK):
            nb = PEER_TOPK // (a + 1)
            cnt_a = jnp.sum(sel[row:row + nb].astype(F32), axis=0, keepdims=True)
            row += nb
            cnt_full = jnp.where(iota == i0[a], cnt_a, cnt_full)
        sl = slice(hd * PEER_KEYS, (hd + 1) * PEER_KEYS)
        r1_ref[sl, cols] = rank1.astype(F32).astype(BF16)
        cnt_ref[sl, cols] = cnt_full
        a_ref[sl, cols] = jnp.exp(s0 - v0[0])
        b_ref[sl, cols] = (jnp.exp(s1 - v1[0]) * inv_z).astype(BF16)


def _peer_dense_kernel(xt_ref, ht_ref, r1_ref, cnt_ref, a_ref, b_ref, u_ref, vt_ref, o_ref, *, rows_per_step):
    c = pl.program_id(1)
    pair = 2 * PEER_KEYS
    n_pairs = rows_per_step // 2

    @pl.when(c == 0)
    def _():
        o_ref[...] = xt_ref[...]

    row_dot = lambda j: jnp.dot(u_ref[j * PEER_KEYS:(j + 1) * PEER_KEYS, :], ht_ref[...],
                                preferred_element_type=F32)
    acc = None
    pre_next = row_dot(0)
    for jp in range(n_pairs):
        parts = []
        for j in range(2 * jp, 2 * jp + 2):
            pre = pre_next
            if j + 1 < rows_per_step:
                pre_next = row_dot(j + 1)
            act = _gelu(pre)
            i1 = c * rows_per_step + j
            g = jnp.zeros(act.shape, BF16)
            for hd in range(PEER_HEADS):
                sl = slice(hd * PEER_KEYS, (hd + 1) * PEER_KEYS)
                cnt_row = cnt_ref[pl.ds(hd * PEER_KEYS + i1, 1), :].astype(BF16)
                a_row = a_ref[pl.ds(hd * PEER_KEYS + i1, 1), :].astype(BF16)
                g = g + jnp.where(r1_ref[sl, :] < cnt_row, b_ref[sl, :] * a_row, jnp.zeros_like(g))
            parts.append(g * act.astype(BF16))
        rows = slice(jp * pair, (jp + 1) * pair)
        part = jnp.dot(vt_ref[:, rows], jnp.concatenate(parts, axis=0), preferred_element_type=F32)
        acc = part if acc is None else acc + part
    o_ref[...] += acc


def _peer_channel_t(xt, g_ffn, wqt, sk, u_bf, vt_bf, tn, rows_per_step=16):
    d, n = xt.shape
    nt = n // tn
    hk = PEER_HEADS * PEER_KEYS
    tok = lambda i: (0, i)
    full = lambda i: (0, 0)
    ht, r1, cnt, a, b = pl.pallas_call(
        _peer_route_kernel,
        grid=(nt,),
        in_specs=[pl.BlockSpec((d, tn), tok), pl.BlockSpec((d, 1), full),
                  pl.BlockSpec(wqt.shape, full), pl.BlockSpec(sk.shape, lambda i: (0, 0, 0))],
        out_specs=[pl.BlockSpec((d, tn), tok)] + [pl.BlockSpec((hk, tn), tok)] * 4,
        out_shape=[jax.ShapeDtypeStruct((d, n), BF16)] + [jax.ShapeDtypeStruct((hk, n), dt) for dt in (BF16, F32, F32, BF16)],
        compiler_params=_cparams("parallel"),
    )(xt, g_ffn.reshape(d, 1), wqt, sk)
    ec = rows_per_step * PEER_KEYS
    n_chunks = u_bf.shape[0] // ec
    tok2 = lambda i, c: (0, i)
    return pl.pallas_call(
        functools.partial(_peer_dense_kernel, rows_per_step=rows_per_step),
        grid=(nt, n_chunks),
        in_specs=[pl.BlockSpec((d, tn), tok2), pl.BlockSpec((d, tn), tok2)]
                 + [pl.BlockSpec((hk, tn), tok2)] * 4
                 + [pl.BlockSpec((ec, d), lambda i, c: (c, 0)), pl.BlockSpec((d, ec), lambda i, c: (0, c))],
        out_specs=pl.BlockSpec((d, tn), tok2),
        out_shape=jax.ShapeDtypeStruct((d, n), F32),
        compiler_params=_cparams("parallel", "arbitrary"),
    )(xt, ht, r1, cnt, a, b, u_bf, vt_bf)


_C_FQ = 0
_C_FKV = _C_FQ + FOX_W
_C_MISC = _C_FKV + 2 * FOX_W
_C_NQ = _C_MISC + LANES
_C_NQR = _C_NQ + NSA_W
_C_NKV = _C_NQR + NSA_W
_C_NKR = _C_NKV + 6 * NSA_KV_W
_C_END = _C_NKR + 3 * NSA_KV_W
N_GATES = 3 * NSA_HEADS


def _rot_cols(w):
    d, c = w.shape
    half = HEAD_DIM // 2
    w4 = w.reshape(d, c // HEAD_DIM, 2, half)
    return jnp.stack([-w4[:, :, 1], w4[:, :, 0]], axis=2).reshape(d, c)


_NSA_PAIR_ORDER = [g * NSA_HPG + j for j in range(NSA_HPG) for g in range(NSA_KV)]


def _pair_perm(n_per_head):
    return np.concatenate([np.arange(h * n_per_head, (h + 1) * n_per_head) for h in _NSA_PAIR_ORDER])


def _prep_w_in(w_in, fox_fb):
    d = w_in.shape[0]
    o = 0
    fq = w_in[:, o:o + FOX_W]; o += FOX_W
    fkv = w_in[:, o:o + 2 * FOX_W]; o += 2 * FOX_W
    ff = w_in[:, o:o + FOX_HEADS]; o += FOX_HEADS
    nq = w_in[:, o:o + NSA_W]; o += NSA_W
    nkv = w_in[:, o:o + 6 * NSA_KV_W]; o += 6 * NSA_KV_W
    ng = w_in[:, o:o + N_GATES]
    misc = jnp.concatenate([ff, ng, jnp.zeros((d, LANES - FOX_HEADS - N_GATES), w_in.dtype)], axis=1)
    nq = nq[:, _pair_perm(HEAD_DIM)]
    nk = nkv.reshape(d, 3, 2, NSA_KV_W)[:, :, 0].reshape(d, 3 * NSA_KV_W)
    w_all = jnp.concatenate([fq, fkv, misc, nq, _rot_cols(nq), nkv, _rot_cols(nk)], axis=1).astype(BF16)
    fb = jnp.concatenate([fox_fb.astype(F32), jnp.zeros((LANES - FOX_HEADS,), F32)]).reshape(1, LANES)
    return w_all, fb


def _rope_tables(pos):
    half = HEAD_DIM // 2
    inv = ROPE_THETA ** (-jnp.arange(half, dtype=F32) / half)
    ang = pos.astype(F32)[:, None] * inv[None, :]
    reps = LANES // half
    return jnp.tile(jnp.cos(ang), (1, reps)), jnp.tile(jnp.sin(ang), (1, reps))


def _proj_kernel(x_ref, g_ref, w_ref, fb_ref, cos_ref, sin_ref,
                 fq_ref, fkv_ref, fkvb_ref, misc_ref, nq_ref, nkv_ref, nkvb_ref):
    x = x_ref[...]
    ms = jnp.mean(x * x, axis=-1, keepdims=True)
    hb = ((x * lax.rsqrt(ms + RMS_EPS)) * g_ref[...]).astype(BF16)

    def mm(c0, c1):
        return jnp.dot(hb, w_ref[:, c0:c1], preferred_element_type=F32)

    fq_ref[...] = (mm(_C_FQ, _C_FKV) * ATTN_SCALE).astype(BF16)
    fkv = mm(_C_FKV, _C_MISC)
    fkv_ref[...] = fkv
    fkvb_ref[...] = fkv.astype(BF16)
    z = mm(_C_MISC, _C_NQ) + fb_ref[...]
    lane = lax.broadcasted_iota(jnp.int32, z.shape, 1)
    log_sig = jnp.minimum(z, 0.0) - jnp.log1p(jnp.exp(-jnp.abs(z)))
    misc_ref[...] = jnp.where(lane < FOX_HEADS, log_sig, jax.nn.sigmoid(z))
    cos = cos_ref[...]
    sin = sin_ref[...]
    reps = NSA_W // LANES
    cos_q = jnp.concatenate([cos] * reps, axis=1)
    sin_q = jnp.concatenate([sin] * reps, axis=1)
    nq = mm(_C_NQ, _C_NQR) * cos_q + mm(_C_NQR, _C_NKV) * sin_q
    nq_ref[...] = (nq * ATTN_SCALE).astype(BF16)
    for s in range(6):
        c0 = _C_NKV + s * NSA_KV_W
        v = mm(c0, c0 + NSA_KV_W)
        if s % 2 == 0:
            r0 = _C_NKR + (s // 2) * NSA_KV_W
            v = v * cos + mm(r0, r0 + NSA_KV_W) * sin
        nkv_ref[:, s * NSA_KV_W:(s + 1) * NSA_KV_W] = v
        nkvb_ref[:, s * NSA_KV_W:(s + 1) * NSA_KV_W] = v.astype(BF16)


def _project(x2, g_mix, w_all, fb, cos, sin, tm):
    n, d = x2.shape
    tok = lambda i: (i, 0)
    full = lambda i: (0, 0)
    widths = [(FOX_W, BF16), (2 * FOX_W, F32), (2 * FOX_W, BF16), (LANES, F32), (NSA_W, BF16),
              (6 * NSA_KV_W, F32), (6 * NSA_KV_W, BF16)]
    return pl.pallas_call(
        _proj_kernel,
        grid=(n // tm,),
        in_specs=[pl.BlockSpec((tm, d), tok), pl.BlockSpec((1, d), full), pl.BlockSpec(w_all.shape, full),
                  pl.BlockSpec((1, LANES), full), pl.BlockSpec((tm, LANES), tok), pl.BlockSpec((tm, LANES), tok)],
        out_specs=[pl.BlockSpec((tm, w), tok) for w, _ in widths],
        out_shape=[jax.ShapeDtypeStruct((n, w), dt) for w, dt in widths],
        compiler_params=_cparams("parallel"),
    )(x2, g_mix.reshape(1, d), w_all, fb, cos, sin)


def _merge_kernel(x_ref, of_ref, on_ref, gf_ref, gn_ref, wf_ref, wn_ref, o_ref):
    def norm(v, g):
        v = v.astype(F32)
        ms = jnp.mean(v * v, axis=-1, keepdims=True)
        return ((v * lax.rsqrt(ms + RMS_EPS)) * g).astype(BF16)

    yf = norm(of_ref[...], gf_ref[...])
    yn = norm(on_ref[...], gn_ref[...])
    o_ref[...] = (x_ref[...] + jnp.dot(yf, wf_ref[...], preferred_element_type=F32)
                  + jnp.dot(yn, wn_ref[...], preferred_element_type=F32))


def _merge(x2, of, on, g_fox, g_nsa, w_f, w_n, tm):
    n, d = x2.shape
    tok = lambda i: (i, 0)
    full = lambda i: (0, 0)
    return pl.pallas_call(
        _merge_kernel,
        grid=(n // tm,),
        in_specs=[pl.BlockSpec((tm, d), tok), pl.BlockSpec((tm, FOX_W), tok), pl.BlockSpec((tm, NSA_W), tok),
                  pl.BlockSpec((1, FOX_W), full), pl.BlockSpec((1, NSA_W), full),
                  pl.BlockSpec((FOX_W, d), full), pl.BlockSpec((NSA_W, d), full)],
        out_specs=pl.BlockSpec((tm, d), tok),
        out_shape=jax.ShapeDtypeStruct((n, d), F32),
        compiler_params=_cparams("parallel"),
    )(x2, of, on, g_fox.reshape(1, -1), g_nsa.reshape(1, -1), w_f, w_n)


_NT = (((1,), (1,)), ((), ()))


def _flash_init(m_ref, l_ref, acc_ref):
    m_ref[...] = jnp.full(m_ref.shape, NEG_INF, F32)
    l_ref[...] = jnp.zeros(l_ref.shape, F32)
    acc_ref[...] = jnp.zeros(acc_ref.shape, F32)


def _flash_steps(qhs, load_k, load_v, bias_fn, lo, hi, tk, m_ref, l_ref, acc_ref):
    def body(kt, carry):
        ks = pl.multiple_of(kt * tk, tk)
        k = load_k(ks)
        v = load_v(ks)
        biases = bias_fn(ks)
        for n, qh in enumerate(qhs):
            s = lax.dot_general(qh, k, _NT, preferred_element_type=F32) + biases[n]
            m_prev = m_ref[n]
            m_new = jnp.maximum(m_prev, jnp.max(s, axis=1, keepdims=True))
            m_safe = jnp.where(m_new == NEG_INF, 0.0, m_new)
            p = jnp.exp(s - _lane_repeat(m_safe, tk))
            alpha = jnp.exp(m_prev - m_safe)
            l_ref[n] = alpha * l_ref[n] + jnp.sum(p, axis=1, keepdims=True)
            acc_ref[n] = alpha * acc_ref[n] + jnp.dot(p.astype(BF16), v, preferred_element_type=F32)
            m_ref[n] = m_new
        return carry

    lax.fori_loop(lo, hi, body, 0)


def _lane_repeat(x, width):
    return jnp.concatenate([x] * (width // LANES), axis=1)


def _flash_finish(l_ref, acc_ref):
    return acc_ref[...] / jnp.maximum(l_ref[...], 1e-30)


def _half_mask(shape, hh):
    lane = lax.broadcasted_iota(jnp.int32, shape, 1)
    return (lane >= HEAD_DIM) if hh else (lane < HEAD_DIM)


def _causal_bias(bias, qstart, ks):
    qpos = qstart + lax.broadcasted_iota(jnp.int32, bias.shape, 0)
    kpos = ks + lax.broadcasted_iota(jnp.int32, bias.shape, 1)
    return jnp.where(kpos <= qpos, bias, NEG_INF)


def _fox_prompt_kernel(q_ref, k_ref, v_ref, cq_ref, ck_ref, o_ref, m_ref, l_ref, acc_ref, *, tq):
    j = pl.program_id(1)
    qi = pl.program_id(2)
    qstart = qi * tq
    qp = q_ref[0]
    load_k = lambda ks: k_ref[0, pl.ds(ks, tq), :]
    load_v = lambda ks: v_ref[0, pl.ds(ks, tq), :]
    halves = [_half_mask(qp.shape, hh) for hh in range(2)]
    qhs = [jnp.where(half, qp, jnp.zeros_like(qp)) for half in halves]
    cqs = [_lane_repeat(jnp.broadcast_to(cq_ref[0, 0, :, hh:hh + 1], (tq, LANES)), tq) for hh in range(2)]
    bias = lambda ks: [cqs[hh] - ck_ref[0, pl.ds(2 * j + hh, 1), pl.ds(ks, tq)] for hh in range(2)]
    bias_diag = lambda ks: [_causal_bias(bb, qstart, ks) for bb in bias(ks)]
    _flash_init(m_ref, l_ref, acc_ref)
    _flash_steps(qhs, load_k, load_v, bias, 0, qi, tq, m_ref, l_ref, acc_ref)
    _flash_steps(qhs, load_k, load_v, bias_diag, qi, qi + 1, tq, m_ref, l_ref, acc_ref)
    o = _flash_finish(l_ref, acc_ref)
    o_ref[0] = jnp.where(halves[0], o[0], o[1])


def _fox_prompt(fq, fkvb, c, tq):
    b, t, _ = fq.shape
    n_pairs = FOX_W // LANES
    cq = c.reshape(b, t, n_pairs, 2).transpose(0, 2, 1, 3)
    ck = c.transpose(0, 2, 1)
    return pl.pallas_call(
        functools.partial(_fox_prompt_kernel, tq=tq),
        grid=(b, n_pairs, t // tq),
        in_specs=[pl.BlockSpec((1, tq, LANES), lambda bi, j, qi: (bi, qi, j)),
                  pl.BlockSpec((1, t, LANES), lambda bi, j, qi: (bi, 0, j)),
                  pl.BlockSpec((1, t, LANES), lambda bi, j, qi: (bi, 0, n_pairs + j)),
                  pl.BlockSpec((1, 1, tq, 2), lambda bi, j, qi: (bi, j, qi, 0)),
                  pl.BlockSpec((1, FOX_HEADS, t), lambda bi, j, qi: (bi, 0, 0))],
        out_specs=pl.BlockSpec((1, tq, LANES), lambda bi, j, qi: (bi, qi, j)),
        out_shape=jax.ShapeDtypeStruct((b, t, FOX_W), F32),
        scratch_shapes=[pltpu.VMEM((2, tq, LANES), F32), pltpu.VMEM((2, tq, LANES), F32), pltpu.VMEM((2, tq, LANES), F32)],
        compiler_params=_cparams("parallel", "parallel", "arbitrary"),
    )(fq, fkvb, fkvb, cq, ck)


def _round_up(x, m):
    return -(-x // m) * m


def _sel_cover_t(nc, nb, ncp):
    cs = np.arange(ncp)[None, :] * CMP_STRIDE
    bs = np.arange(nb)[:, None] * SEL_BLOCK
    cov = np.clip(np.minimum(cs + CMP_LEN, bs + SEL_BLOCK) - np.maximum(cs, bs), 0, None)
    cov = np.where(np.arange(ncp)[None, :] < nc, cov, 0)
    return jnp.asarray(cov.astype(np.float32) / CMP_LEN, BF16)


def _nsa_prompt_kernel(q_ref, kv_ref, kc_ref, vc_ref, misc_ref, covt_ref, o_ref,
                       qh_ref, oc_ref, bias_ref, m_ref, l_ref, acc_ref, *, tq, nc, nb, n_sel):
    qi = pl.program_id(1)
    qstart = qi * tq
    ncp = kc_ref.shape[1]
    n_pairs = NSA_W // LANES
    load = lambda slot: (lambda ks: kv_ref[0, pl.ds(ks, tq), slot * LANES:(slot + 1) * LANES])

    cidx = lax.broadcasted_iota(jnp.int32, (tq, ncp), 1)
    qpos_c = qstart + lax.broadcasted_iota(jnp.int32, (tq, ncp), 0)
    cmask = (cidx * CMP_STRIDE + (CMP_LEN - 1) <= qpos_c) & (cidx < nc)
    psum = [jnp.zeros((tq, ncp), F32) for _ in range(NSA_KV)]
    for j in range(n_pairs):
        qp = q_ref[0, :, j * LANES:(j + 1) * LANES]
        for g in range(NSA_KV):
            h = j * NSA_KV + g
            qh = jnp.where(_half_mask(qp.shape, g), qp, jnp.zeros_like(qp))
            qh_ref[h] = qh
            s = lax.dot_general(qh, kc_ref[0], _NT, preferred_element_type=F32)
            s = jnp.where(cmask, s, NEG_INF)
            m = jnp.max(s, axis=1, keepdims=True)
            m = jnp.where(m == NEG_INF, 0.0, m)
            e = jnp.exp(s - m)
            p = e / jnp.maximum(jnp.sum(e, axis=1, keepdims=True), 1e-30)
            psum[g] = psum[g] + p
            oc_ref[h] = jnp.dot(p.astype(BF16), vc_ref[0], preferred_element_type=F32)

    for g in range(NSA_KV):
        hi = psum[g].astype(BF16)
        lo = (psum[g] - hi.astype(F32)).astype(BF16)
        pb = (lax.dot_general(covt_ref[...], hi, _NT, preferred_element_type=F32)
              + lax.dot_general(covt_ref[...], lo, _NT, preferred_element_type=F32))
        jj = lax.broadcasted_iota(jnp.int32, (nb, tq), 0)
        cur = (qstart + lax.broadcasted_iota(jnp.int32, (nb, tq), 1)) // SEL_BLOCK
        causal = jj <= cur
        forced = (jj == 0) | (jj == cur) | (jj == cur - 1)
        score = jnp.where(forced, FORCED_SCORE, jnp.where(causal, pb, NEG_INF))
        _, _, rank = _extract_topk(score, n_sel)
        sel_t = jnp.where((rank < n_sel) & causal, 1.0, 0.0)
        sel_t = jnp.concatenate([sel_t, jnp.zeros((LANES - nb, tq), F32)], axis=0)
        sel = sel_t.T.astype(BF16)

        def fill(kt, carry):
            ks = pl.multiple_of(kt * tq, tq)
            blk = (ks + lax.broadcasted_iota(jnp.int32, (LANES, tq), 1)) // SEL_BLOCK
            expand = jnp.where(lax.broadcasted_iota(jnp.int32, (LANES, tq), 0) == blk, 1.0, 0.0).astype(BF16)
            hit = jnp.dot(sel, expand, preferred_element_type=F32)
            bias = jnp.where(hit > 0.5, 0.0, NEG_INF)
            bias_ref[g, :, pl.ds(ks, tq)] = _causal_bias(bias, qstart, ks)
            return carry

        lax.fori_loop(0, qi + 1, fill, 0)

    def win_bias(ks):
        qpos = qstart + lax.broadcasted_iota(jnp.int32, (tq, tq), 0)
        kpos = ks + lax.broadcasted_iota(jnp.int32, (tq, tq), 1)
        dist = qpos - kpos
        return jnp.where((dist >= 0) & (dist < WINDOW), 0.0, NEG_INF)

    win_lo = jnp.maximum(qi - (-(-WINDOW // tq)), 0)
    sel_bias = lambda ks: [bias_ref[g, :, pl.ds(ks, tq)] for g in range(NSA_KV)]
    win_bias2 = lambda ks: [win_bias(ks)] * NSA_KV
    for j in range(n_pairs):
        qhs = [qh_ref[j * NSA_KV + g] for g in range(NSA_KV)]
        _flash_init(m_ref, l_ref, acc_ref)
        _flash_steps(qhs, load(2), load(3), sel_bias, 0, qi + 1, tq, m_ref, l_ref, acc_ref)
        o_s = _flash_finish(l_ref, acc_ref)
        _flash_init(m_ref, l_ref, acc_ref)
        _flash_steps(qhs, load(4), load(5), win_bias2, win_lo, qi + 1, tq, m_ref, l_ref, acc_ref)
        o_w = _flash_finish(l_ref, acc_ref)
        outs = []
        for g in range(NSA_KV):
            col = FOX_HEADS + (g * NSA_HPG + j) * 3
            gates = [misc_ref[0, :, col + r:col + r + 1] for r in range(3)]
            outs.append(gates[0] * oc_ref[j * NSA_KV + g] + gates[1] * o_s[g] + gates[2] * o_w[g])
        o_ref[0, :, j * LANES:(j + 1) * LANES] = jnp.where(_half_mask(outs[0].shape, 0), outs[0], outs[1])


def _nsa_prompt(nq, nkvb, kc, vc, misc, tq):
    b, t, _ = nq.shape
    nc = kc.shape[1] - 1
    ncp = _round_up(nc + 1, LANES)
    nb = -(-t // SEL_BLOCK)
    assert nb <= LANES and t % tq == 0 and tq % SEL_BLOCK == 0
    kcp = _pad_rows(kc, ncp)
    vcp = _pad_rows(vc, ncp)
    covt = _sel_cover_t(nc, nb, ncp)
    kern = functools.partial(_nsa_prompt_kernel, tq=tq, nc=nc, nb=nb, n_sel=min(N_SELECT, nb))
    seq = lambda bi, qi: (bi, 0, 0)
    tile = lambda bi, qi: (bi, qi, 0)
    return pl.pallas_call(
        kern,
        grid=(b, t // tq),
        in_specs=[pl.BlockSpec((1, tq, NSA_W), tile), pl.BlockSpec((1, t, 6 * NSA_KV_W), seq),
                  pl.BlockSpec((1, ncp, LANES), seq), pl.BlockSpec((1, ncp, LANES), seq),
                  pl.BlockSpec((1, tq, LANES), tile), pl.BlockSpec((nb, ncp), lambda bi, qi: (0, 0))],
        out_specs=pl.BlockSpec((1, tq, NSA_W), tile),
        out_shape=jax.ShapeDtypeStruct((b, t, NSA_W), F32),
        scratch_shapes=[pltpu.VMEM((NSA_HEADS, tq, LANES), BF16), pltpu.VMEM((NSA_HEADS, tq, LANES), F32),
                        pltpu.VMEM((NSA_KV, tq, t), F32),
                        pltpu.VMEM((NSA_KV, tq, LANES), F32), pltpu.VMEM((NSA_KV, tq, LANES), F32),
                        pltpu.VMEM((NSA_KV, tq, LANES), F32)],
        compiler_params=_cparams("parallel", "arbitrary"),
    )(nq, nkvb, kcp, vcp, misc, covt)


def _tile(n, pref):
    t = min(pref, n)
    while n % t:
        t //= 2
    return t


def _mixer_prompt(x, g_mix, w_all, fb, cmp_weights):
    b, t, d = x.shape
    n = b * t
    cos, sin = _rope_tables(jnp.arange(t))
    cos = jnp.tile(cos, (b, 1))
    sin = jnp.tile(sin, (b, 1))
    fq, fkv, fkvb, misc, nq, nkv, nkvb = _project(x.reshape(n, d), g_mix, w_all, fb, cos, sin, _tile(n, 512))
    logf = misc[:, :FOX_HEADS].reshape(b, t, FOX_HEADS)
    c = jnp.cumsum(logf, axis=1)
    tq = _tile(t, 512)
    of = _fox_prompt(fq.reshape(b, t, FOX_W), fkvb.reshape(b, t, 2 * FOX_W), c, tq)
    rows = nkv.reshape(b, t, 6, NSA_KV, HEAD_DIM)
    page_rows = _tile(t, LANES)
    pages = jnp.arange(n // page_rows, dtype=jnp.int32).reshape(b, t // page_rows)
    kc, vc = _nsa_cmp_sample(0, pages, nkv.reshape(1, n // page_rows, page_rows, 6 * NSA_KV_W), *cmp_weights,
                             _tile(t // page_rows, 16), row_major=True)
    on = _nsa_prompt(nq.reshape(b, t, NSA_W), nkvb.reshape(b, t, 6 * NSA_KV_W), kc, vc, misc.reshape(b, t, LANES), tq)
    wb = min(WINDOW, t)
    return (of, on, fkv.reshape(b, t, 2, FOX_HEADS, HEAD_DIM), logf, rows[:, :, :4], rows[:, t - wb:, 4:])


def _split3(x):
    hi = x.astype(BF16)
    r = x - hi.astype(F32)
    mid = r.astype(BF16)
    return hi, mid, (r - mid.astype(F32)).astype(BF16)


def _split2(x):
    hi = x.astype(BF16)
    return hi, (x - hi.astype(F32)).astype(BF16)


def _softmax_update(s_list, v_list, m_ref, l_ref, acc_ref, v_transposed=False):
    pv = (lambda p, v: lax.dot_general(p, v, _NT, preferred_element_type=F32)) if v_transposed else (
        lambda p, v: jnp.dot(p, v, preferred_element_type=F32))
    m_prev = m_ref[...]
    m_new = m_prev
    for s in s_list:
        m_new = jnp.maximum(m_new, jnp.max(s, axis=1, keepdims=True))
    m_safe = jnp.where(m_new == NEG_INF, 0.0, m_new)
    alpha = jnp.exp(m_prev - m_safe)
    l_new = alpha * l_ref[...]
    acc = alpha * acc_ref[...]
    for s, v_bf in zip(s_list, v_list):
        p = jnp.exp(s - m_safe)
        l_new = l_new + jnp.sum(p, axis=1, keepdims=True)
        acc = acc + pv(p.astype(BF16), v_bf)
    l_ref[...] = l_new
    acc_ref[...] = acc
    m_ref[...] = m_new


def _fox_sample_kernel(pt_ref, q_ref, cnew_ref, knew_ref, vnew_ref, bnew_ref, scan_ref, *rest, n_pg, s_tok):
    kv_refs, lf_refs = rest[:n_pg], rest[n_pg:2 * n_pg]
    o_ref, m_ref, l_ref, acc_ref, carry_ref = rest[2 * n_pg:]
    st = pl.program_id(1)

    @pl.when(st == 0)
    def _():
        _flash_init(m_ref, l_ref, acc_ref)
        carry_ref[...] = jnp.zeros(carry_ref.shape, F32)

    q = q_ref[0]
    cnew = cnew_ref[0]
    page_rows = kv_refs[0].shape[2]
    carry = carry_ref[...]
    s_list, v_list = [], []
    for i in reversed(range(n_pg)):
        x = lf_refs[i][...]
        sc = sum(jnp.dot(p, scan_ref[...], preferred_element_type=F32) for p in _split3(x))
        d = sc[:, :page_rows] + carry
        carry = carry + sc[:, page_rows:]
        dd = jnp.concatenate([jnp.broadcast_to(d[h:h + 1], (s_tok, page_rows)) for h in range(FOX_HEADS)], axis=0)
        s_list.append(jnp.dot(q, kv_refs[i][0].astype(BF16), preferred_element_type=F32) + (cnew + dd))
        v_list.append(kv_refs[i][1].astype(BF16))
    carry_ref[...] = carry
    _softmax_update(s_list, v_list, m_ref, l_ref, acc_ref, v_transposed=True)

    @pl.when(st == pl.num_programs(1) - 1)
    def _():
        s = lax.dot_general(q, knew_ref[0], _NT, preferred_element_type=F32) + bnew_ref[0]
        _softmax_update([s], [vnew_ref[0]], m_ref, l_ref, acc_ref)
        o = _flash_finish(l_ref, acc_ref)
        lane_head = lax.broadcasted_iota(jnp.int32, (s_tok, FOX_W), 1) // HEAD_DIM
        out = jnp.zeros((s_tok, FOX_W), F32)
        for h in range(FOX_HEADS):
            out = jnp.where(lane_head == h, o[h * s_tok:(h + 1) * s_tok], out)
        o_ref[0] = out


def _fox_sample(layer, page_table, q_rows, cnew_b, knew, vnew, bnew, cache_kv, logf_t, n_pg):
    bd, n_pages = page_table.shape
    page_rows = cache_kv.shape[4]
    s_tok = q_rows.shape[1] // FOX_HEADS
    rows = q_rows.shape[1]
    kidx = np.arange(page_rows)
    scan = np.concatenate([(kidx[:, None] > kidx[None, :]), np.ones((page_rows, page_rows), bool)], axis=1)
    scan = jnp.asarray(scan, BF16)
    n_st = n_pages // n_pg
    seq = lambda b, st, pt: (b, 0, 0)

    def page_map(i, nd):
        return lambda b, st, pt: (layer, pt[b, n_pages - (st + 1) * n_pg + i]) + (0,) * nd

    grid_spec = pltpu.PrefetchScalarGridSpec(
        num_scalar_prefetch=1,
        grid=(bd, n_st),
        in_specs=[pl.BlockSpec((1, rows, FOX_W), seq), pl.BlockSpec((1, rows, page_rows), seq),
                  pl.BlockSpec((1, page_rows, FOX_W), seq), pl.BlockSpec((1, page_rows, FOX_W), seq),
                  pl.BlockSpec((1, rows, page_rows), seq),
                  pl.BlockSpec(scan.shape, lambda b, st, pt: (0, 0))]
                 + [pl.BlockSpec((None, None, 2, FOX_W, page_rows), page_map(i, 3)) for i in range(n_pg)]
                 + [pl.BlockSpec((None, None, FOX_HEADS, page_rows), page_map(i, 2)) for i in range(n_pg)],
        out_specs=pl.BlockSpec((1, s_tok, FOX_W), seq),
        scratch_shapes=[pltpu.VMEM((rows, 1), F32), pltpu.VMEM((rows, 1), F32), pltpu.VMEM((rows, FOX_W), F32),
                        pltpu.VMEM((FOX_HEADS, page_rows), F32)])
    return pl.pallas_call(
        functools.partial(_fox_sample_kernel, n_pg=n_pg, s_tok=s_tok),
        grid_spec=grid_spec,
        out_shape=jax.ShapeDtypeStruct((bd, s_tok, FOX_W), F32),
        compiler_params=_cparams("parallel", "arbitrary"),
    )(page_table, q_rows, cnew_b, knew, vnew, bnew, scan, *([cache_kv] * n_pg), *([logf_t] * n_pg))


def _nsa_cmp_sample_kernel(pt_ref, pe_ref, w1_ref, w2_ref, *rest, n_pg, row_major):
    pages = rest[:n_pg]
    kc_ref, vc_ref, carry_ref, xs_ref = rest[n_pg:]
    st = pl.program_id(1)

    @pl.when(st == 0)
    def _():
        carry_ref[...] = jnp.zeros(carry_ref.shape, F32)

    page_rows = pages[0].shape[0] if row_major else pages[0].shape[2]
    m = n_pg * page_rows // CMP_STRIDE
    last = lax.broadcasted_iota(jnp.int32, (m, NSA_KV_W), 0) == m - 1
    for slot, out_ref in ((0, kc_ref), (1, vc_ref)):
        for i, pg in enumerate(pages):
            xs_ref[i * page_rows:(i + 1) * page_rows, :] = (
                pg[:, slot * NSA_KV_W:(slot + 1) * NSA_KV_W] if row_major else pg[slot].T)
        a = jnp.zeros((m, NSA_KV_W), F32)
        b = jnp.zeros((m, NSA_KV_W), F32)
        for r in range(CMP_STRIDE):
            x = xs_ref[pl.ds(r, m, stride=CMP_STRIDE), :]
            xa = (x + pe_ref[slot, r:r + 1, :]).astype(BF16)
            xb = (x + pe_ref[slot, CMP_STRIDE + r:CMP_STRIDE + r + 1, :]).astype(BF16)
            a = a + jnp.dot(xa, w1_ref[slot, r], preferred_element_type=F32)
            b = b + jnp.dot(xb, w1_ref[slot, CMP_STRIDE + r], preferred_element_type=F32)
        b_next = jnp.where(last, carry_ref[slot, 0:1, :], pltpu.roll(b, m - 1, axis=0))
        carry_ref[slot] = b
        h = _gelu(a + b_next).astype(BF16)
        out_ref[0] = jnp.dot(h, w2_ref[slot], preferred_element_type=F32).astype(BF16)


def _nsa_cmp_sample(layer, page_table, cache_nsa, pe2, w1_bd, w2_bd, n_pg, row_major=False):
    bd, n_pages = page_table.shape
    page_rows = cache_nsa.shape[2] if row_major else cache_nsa.shape[4]
    cpp = page_rows // CMP_STRIDE
    m = n_pg * cpp
    n_st = n_pages // n_pg
    const = lambda nd: (lambda b, st, pt: (0,) * nd)

    def page_map(i):
        return lambda b, st, pt: (layer, pt[b, n_pages - (st + 1) * n_pg + i]) + (0,) * (2 if row_major else 3)

    page_block = (None, None, page_rows, 2 * NSA_KV_W) if row_major else (None, None, 2) + cache_nsa.shape[3:]
    out_spec = pl.BlockSpec((1, m, NSA_KV_W), lambda b, st, pt: (b, n_st - 1 - st, 0))
    grid_spec = pltpu.PrefetchScalarGridSpec(
        num_scalar_prefetch=1,
        grid=(bd, n_st),
        in_specs=[pl.BlockSpec(pe2.shape, const(3)), pl.BlockSpec(w1_bd.shape, const(4)),
                  pl.BlockSpec(w2_bd.shape, const(3))]
                 + [pl.BlockSpec(page_block, page_map(i)) for i in range(n_pg)],
        out_specs=[out_spec, out_spec],
        scratch_shapes=[pltpu.VMEM((2, m, NSA_KV_W), F32), pltpu.VMEM((n_pg * page_rows, NSA_KV_W), F32)])
    shape = jax.ShapeDtypeStruct((bd, n_pages * cpp, NSA_KV_W), BF16)
    return pl.pallas_call(
        functools.partial(_nsa_cmp_sample_kernel, n_pg=n_pg, row_major=row_major),
        grid_spec=grid_spec,
        out_shape=[shape, shape],
        compiler_params=_cparams("parallel", "arbitrary"),
    )(page_table, pe2, w1_bd, w2_bd, *([cache_nsa] * n_pg))


def _nsa_sample_kernel(pt_ref, q_ref, kc_ref, vc_ref, gate_ref, covt_ref, gsum_ref, win_ref, new_ref, *rest,
                       n_pg, s_tok, nc, nb, n_sel, past):
    pages = rest[:n_pg]
    o_ref, bias_ref, oc_ref, m_ref, l_ref, acc_ref = rest[n_pg:]
    st = pl.program_id(1)
    q = q_ref[0]
    rows = q.shape[0]
    page_rows = pages[0].shape[2]

    @pl.when(st == 0)
    def _():
        ncp = kc_ref.shape[1]
        nbp = covt_ref.shape[0]
        s = lax.dot_general(q, kc_ref[0], _NT, preferred_element_type=F32)
        cidx = lax.broadcasted_iota(jnp.int32, (rows, ncp), 1)
        qpos = past + lax.broadcasted_iota(jnp.int32, (rows, ncp), 0) % s_tok
        s = jnp.where((cidx * CMP_STRIDE + (CMP_LEN - 1) <= qpos) & (cidx < nc), s, NEG_INF)
        m = jnp.max(s, axis=1, keepdims=True)
        m = jnp.where(m == NEG_INF, 0.0, m)
        e = jnp.exp(s - m)
        p = e / jnp.maximum(jnp.sum(e, axis=1, keepdims=True), 1e-30)
        oc_ref[...] = jnp.dot(p.astype(BF16), vc_ref[0], preferred_element_type=F32)
        psum = sum(jnp.dot(gsum_ref[...], x, preferred_element_type=F32) for x in _split2(p))
        pb = sum(lax.dot_general(covt_ref[...], x, _NT, preferred_element_type=F32) for x in _split2(psum))
        jj = lax.broadcasted_iota(jnp.int32, (nbp, rows), 0)
        cur = (past + lax.broadcasted_iota(jnp.int32, (nbp, rows), 1) % s_tok) // SEL_BLOCK
        causal = (jj <= cur) & (jj < nb)
        forced = (jj == 0) | (jj == cur) | (jj == cur - 1)
        score = jnp.where(forced, FORCED_SCORE, jnp.where(causal, pb, NEG_INF))
        _, _, rank = _extract_topk(score, n_sel)
        sel = jnp.where((rank < n_sel) & causal, 1.0, 0.0).T.astype(BF16)
        chunk = min(past, 16 * page_rows)

        def fill(c, carry):
            ks = pl.multiple_of(c * chunk, chunk)
            blk = (ks + lax.broadcasted_iota(jnp.int32, (nbp, chunk), 1)) // SEL_BLOCK
            expand = jnp.where(lax.broadcasted_iota(jnp.int32, (nbp, chunk), 0) == blk, 1.0, 0.0).astype(BF16)
            hit = jnp.dot(sel, expand, preferred_element_type=F32)
            bias_ref[:, pl.ds(ks, chunk)] = jnp.where(hit > 0.5, 0.0, NEG_INF)
            return carry

        lax.fori_loop(0, bias_ref.shape[1] // chunk, fill, 0)
        _flash_init(m_ref, l_ref, acc_ref)

    s_list, v_list = [], []
    for i in range(n_pg):
        ks = pl.multiple_of((st * n_pg + i) * page_rows, page_rows)
        s = jnp.dot(q, pages[i][0].astype(BF16), preferred_element_type=F32)
        s_list.append(s + bias_ref[:, pl.ds(ks, page_rows)])
        v_list.append(pages[i][1].astype(BF16))
    _softmax_update(s_list, v_list, m_ref, l_ref, acc_ref, v_transposed=True)

    @pl.when(st == pl.num_programs(1) - 1)
    def _():
        tok = lax.broadcasted_iota(jnp.int32, (rows, page_rows), 0) % s_tok
        col = lax.broadcasted_iota(jnp.int32, (rows, page_rows), 1)
        bnew = jnp.where((col <= tok) & (col < s_tok), 0.0, NEG_INF)
        s = lax.dot_general(q, new_ref[0, 0], _NT, preferred_element_type=F32) + bnew
        _softmax_update([s], [new_ref[0, 1]], m_ref, l_ref, acc_ref)
        o_s = _flash_finish(l_ref, acc_ref)
        wb = win_ref.shape[2]
        wtok = lax.broadcasted_iota(jnp.int32, (rows, wb), 0) % s_tok
        wi = lax.broadcasted_iota(jnp.int32, (rows, wb), 1)
        dist = wb + wtok - wi
        ok = (past - wb + wi >= 0) & (dist >= 0) & (dist < WINDOW)
        s1 = jnp.dot(q, win_ref[0].astype(BF16), preferred_element_type=F32)
        s1 = jnp.where(ok, s1, NEG_INF)
        s2 = lax.dot_general(q, new_ref[0, 2], _NT, preferred_element_type=F32) + bnew
        m = jnp.maximum(jnp.max(s1, axis=1, keepdims=True), jnp.max(s2, axis=1, keepdims=True))
        e1 = jnp.exp(s1 - m)
        e2 = jnp.exp(s2 - m)
        den = jnp.sum(e1, axis=1, keepdims=True) + jnp.sum(e2, axis=1, keepdims=True)
        o_w = (lax.dot_general(e1.astype(BF16), win_ref[1].astype(BF16), _NT, preferred_element_type=F32)
               + jnp.dot(e2.astype(BF16), new_ref[0, 3], preferred_element_type=F32)) / den
        o = gate_ref[0, 0] * oc_ref[...] + gate_ref[0, 1] * o_s + gate_ref[0, 2] * o_w
        for j in range(NSA_W // LANES):
            lo = o[(2 * j) * s_tok:(2 * j + 1) * s_tok]
            hi = o[(2 * j + 1) * s_tok:(2 * j + 2) * s_tok]
            o_ref[0, :, j * LANES:(j + 1) * LANES] = jnp.where(_half_mask(lo.shape, 0), lo, hi)


def _nsa_sample(layer, page_table, q_rows, kc, vc, gates, win_state, new_kv, cache_nsa, s_tok, n_pg):
    bd, n_pages = page_table.shape
    page_rows = cache_nsa.shape[4]
    past = n_pages * page_rows
    rows = q_rows.shape[1]
    ncp = kc.shape[1]
    nc = ncp - 1
    nb = -(-(past + s_tok) // SEL_BLOCK)
    nbp = _round_up(nb, LANES)
    assert past % SEL_BLOCK == 0 and s_tok <= SEL_BLOCK and s_tok < CMP_STRIDE
    covt = _sel_cover_t(nc, nbp, ncp)
    r = np.arange(rows)
    same = (r[:, None] % s_tok == r[None, :] % s_tok) & ((r[:, None] // s_tok) % NSA_KV == (r[None, :] // s_tok) % NSA_KV)
    live = r < NSA_HEADS * s_tok
    gsum = jnp.asarray(same & live[:, None] & live[None, :], BF16)
    wb = win_state.shape[4]
    n_st = n_pages // n_pg
    seq = lambda b, st, pt: (b, 0, 0)
    const2 = lambda b, st, pt: (0, 0)
    grid_spec = pltpu.PrefetchScalarGridSpec(
        num_scalar_prefetch=1,
        grid=(bd, n_st),
        in_specs=[pl.BlockSpec((1, rows, LANES), seq), pl.BlockSpec((1, ncp, LANES), seq),
                  pl.BlockSpec((1, ncp, LANES), seq), pl.BlockSpec((1, 3, rows, LANES), lambda b, st, pt: (b, 0, 0, 0)),
                  pl.BlockSpec(covt.shape, const2), pl.BlockSpec(gsum.shape, const2),
                  pl.BlockSpec((None, None, 2, NSA_KV_W, wb), lambda b, st, pt: (layer, b, 0, 0, 0)),
                  pl.BlockSpec((1, 4, page_rows, LANES), lambda b, st, pt: (b, 0, 0, 0))]
                 + [pl.BlockSpec((None, None, 2) + cache_nsa.shape[3:],
                                 (lambda i: lambda b, st, pt: (layer, pt[b, st * n_pg + i], 1, 0, 0))(i))
                    for i in range(n_pg)],
        out_specs=pl.BlockSpec((1, s_tok, NSA_W), seq),
        scratch_shapes=[pltpu.VMEM((rows, past), F32), pltpu.VMEM((rows, LANES), F32),
                        pltpu.VMEM((rows, 1), F32), pltpu.VMEM((rows, 1), F32), pltpu.VMEM((rows, LANES), F32)])
    kern = functools.partial(_nsa_sample_kernel, n_pg=n_pg, s_tok=s_tok, nc=nc, nb=nb, n_sel=min(N_SELECT, nb),
                             past=past)
    return pl.pallas_call(
        kern,
        grid_spec=grid_spec,
        out_shape=jax.ShapeDtypeStruct((bd, s_tok, NSA_W), F32),
        compiler_params=_cparams("parallel", "arbitrary"),
    )(page_table, q_rows, kc, vc, gates, covt, gsum, win_state, new_kv, *([cache_nsa] * n_pg))


def _block_diag_groups(w):
    eye = jnp.eye(NSA_KV, dtype=w.dtype)
    out = jnp.einsum('gh,...de->...gdhe', eye, w)
    return out.reshape(w.shape[:-2] + (NSA_KV_W, NSA_KV_W))


def _pad_rows(x, rows, value=0.0):
    pad = [(0, 0)] * x.ndim
    pad[-2] = (0, rows - x.shape[-2])
    return jnp.pad(x, pad, constant_values=value)


def _cmp_weights(cmp_pe, cmp_w1, cmp_w2):
    pe2 = jnp.tile(cmp_pe, (1, 1, NSA_KV))
    w1_bd = _block_diag_groups(cmp_w1.reshape(2, CMP_LEN, HEAD_DIM, HEAD_DIM)).astype(BF16)
    return pe2, w1_bd, _block_diag_groups(cmp_w2).astype(BF16)


def _mixer_sample(x, layer, page_table, cache_fox_kv, logf_t, cache_nsa_kv, state_win, state_win_rows, g_mix, w_all,
                  fb, cmp_weights):
    bd, s, d = x.shape
    n_pages = page_table.shape[1]
    page_rows = cache_fox_kv.shape[4]
    past = n_pages * page_rows
    n = bd * s
    pos = past + jnp.arange(s)
    cos, sin = _rope_tables(pos)
    cos = jnp.tile(cos, (bd, 1))
    sin = jnp.tile(sin, (bd, 1))
    fq, fkv, fkvb, misc, nq, nkv, nkvb = _project(x.reshape(n, d), g_mix, w_all, fb, cos, sin, _tile(n, 256))
    logf = misc[:, :FOX_HEADS].reshape(bd, s, FOX_HEADS)
    cnew = jnp.cumsum(logf, axis=1).transpose(0, 2, 1)
    rows_f = FOX_HEADS * s
    cnew_b = jnp.broadcast_to(cnew.reshape(bd, rows_f, 1), (bd, rows_f, page_rows))
    causal = jnp.arange(s)[None, :] <= jnp.arange(s)[:, None]
    bnew = jnp.where(causal, cnew[:, :, :, None] - cnew[:, :, None, :], NEG_INF).reshape(bd, rows_f, s)
    bnew = jnp.pad(bnew, ((0, 0), (0, 0), (0, page_rows - s)), constant_values=NEG_INF)
    head_of_lane = jnp.arange(FOX_W) // HEAD_DIM
    fq3 = fq.reshape(bd, 1, s, FOX_W)
    q_rows = jnp.where(head_of_lane[None, None, None, :] == jnp.arange(FOX_HEADS)[None, :, None, None], fq3,
                       jnp.zeros_like(fq3)).reshape(bd, rows_f, FOX_W)
    fkvb3 = fkvb.reshape(bd, s, 2 * FOX_W)
    knew = _pad_rows(fkvb3[:, :, :FOX_W], page_rows)
    vnew = _pad_rows(fkvb3[:, :, FOX_W:], page_rows)
    of = _fox_sample(layer, page_table, q_rows, cnew_b, knew, vnew, bnew, cache_fox_kv, logf_t, _tile(n_pages, 16))
    n_pg = _tile(n_pages, 16)
    kc, vc = _nsa_cmp_sample(layer, page_table, cache_nsa_kv, *cmp_weights, n_pg)
    rows_n = _round_up(NSA_HEADS * s, LANES)
    nq4 = nq.reshape(bd, s, NSA_W // LANES, 1, LANES).transpose(0, 2, 3, 1, 4)
    half_of_lane = (jnp.arange(LANES) // HEAD_DIM)[None, None, None, None, :]
    qn = jnp.where(half_of_lane == jnp.arange(NSA_KV)[None, None, :, None, None], nq4, jnp.zeros_like(nq4))
    qn = _pad_rows(qn.reshape(bd, NSA_HEADS * s, LANES), rows_n)
    gate = misc[:, FOX_HEADS:FOX_HEADS + N_GATES].reshape(bd, s, NSA_KV, NSA_HPG, 3)
    gate = gate.transpose(0, 4, 3, 2, 1).reshape(bd, 3, NSA_HEADS * s, 1)
    gates = jnp.broadcast_to(_pad_rows(gate, rows_n), (bd, 3, rows_n, LANES))
    nkvb3 = nkvb.reshape(bd, s, 6, NSA_KV_W)
    new_kv = _pad_rows(nkvb3[:, :, 2:6].transpose(0, 2, 1, 3), page_rows)
    on = _nsa_sample(layer, page_table, qn, kc, vc, gates, state_win, new_kv, cache_nsa_kv, s, n_pg)
    rows = nkv.reshape(bd, s, 6, NSA_KV, HEAD_DIM)
    win_all = jnp.concatenate([state_win_rows, rows[:, :, 4:]], axis=1)
    return of, on, fkv.reshape(bd, s, 2, FOX_HEADS, HEAD_DIM), logf, rows[:, :, :4], win_all[:, s:]


def _rmsnorm_kernel(x_ref, g_ref, o_ref):
    x = x_ref[...]
    ms = jnp.mean(x * x, axis=-1, keepdims=True)
    o_ref[...] = (x * lax.rsqrt(ms + RMS_EPS)) * g_ref[...]


def _rmsnorm(x2, g, tm):
    n, d = x2.shape
    return pl.pallas_call(
        _rmsnorm_kernel,
        grid=(n // tm,),
        in_specs=[pl.BlockSpec((tm, d), lambda i: (i, 0)), pl.BlockSpec((1, d), lambda i: (0, 0))],
        out_specs=pl.BlockSpec((tm, d), lambda i: (i, 0)),
        out_shape=jax.ShapeDtypeStruct((n, d), F32),
        compiler_params=_cparams("parallel"),
    )(x2, g.reshape(1, d))


def kernel(x_prompt, x_sample, cache_fox_kv, cache_fox_logf, cache_nsa_kv, state_win_kv, page_table,
           norm_mix, w_in, fox_fb, cmp_pe, cmp_w1, cmp_w2, norm_fox, norm_nsa, w_out,
           norm_ffn, peer_wq, peer_subkeys, peer_u, peer_v, norm_final):
    depth = w_in.shape[0]
    b, t, d = x_prompt.shape
    bd, s, _ = x_sample.shape
    n_seq, n_pages = page_table.shape
    past = n_pages * cache_fox_kv.shape[2]
    np_, ns = b * t, bd * s
    xp = x_prompt.reshape(np_, d)
    xs = x_sample.reshape(ns, d)
    perm = _pair_perm(HEAD_DIM)
    outs = [[] for _ in range(8)]
    pool, page_rows = cache_fox_kv.shape[1:3]
    row_minor = (0, 1, 3, 4, 5, 2)
    fox_kv2 = cache_fox_kv.transpose(row_minor).reshape(depth, pool, 2, FOX_W, page_rows)
    logf_t = cache_fox_logf.transpose(0, 1, 3, 2)
    nsa_kv2 = cache_nsa_kv.transpose(row_minor).reshape(depth, pool, 4, NSA_KV_W, page_rows)
    win2 = state_win_kv.transpose(row_minor).reshape(depth, bd, 2, NSA_KV_W, state_win_kv.shape[2])
    for l in range(depth):
        w_all, fb = _prep_w_in(w_in[l], fox_fb[l])
        cmp_weights = _cmp_weights(cmp_pe[l], cmp_w1[l], cmp_w2[l])
        mp = _mixer_prompt(xp.reshape(b, t, d), norm_mix[l], w_all, fb, cmp_weights)
        msm = _mixer_sample(xs.reshape(bd, s, d), l, page_table, fox_kv2, logf_t, nsa_kv2, win2, state_win_kv[l],
                            norm_mix[l], w_all, fb, cmp_weights)
        w_f = w_out[l, :FOX_W].astype(BF16)
        w_n = w_out[l, FOX_W:][perm].astype(BF16)
        g_n = norm_nsa[l][perm]
        xp = _merge(xp, mp[0].reshape(np_, FOX_W), mp[1].reshape(np_, NSA_W), norm_fox[l], g_n, w_f, w_n,
                    _tile(np_, 512))
        xs = _merge(xs, msm[0].reshape(ns, FOX_W), msm[1].reshape(ns, NSA_W), norm_fox[l], g_n, w_f, w_n,
                    _tile(ns, 256))
        wqt = peer_wq[l].T.astype(BF16)
        sk = peer_subkeys[l].astype(BF16)
        u_bf = peer_u[l].astype(BF16)
        vt_bf = peer_v[l].T.astype(BF16)
        xp = _peer_channel_t(xp.T, norm_ffn[l], wqt, sk, u_bf, vt_bf, _tile(np_, 512)).T
        xs = _peer_channel_t(xs.T, norm_ffn[l], wqt, sk, u_bf, vt_bf, _tile(ns, 256)).T
        for i in range(4):
            outs[i].append(mp[2 + i])
            outs[4 + i].append(msm[2 + i])
    y_prompt = _rmsnorm(xp, norm_final, _tile(np_, 512)).reshape(b, t, d)
    y_sample = _rmsnorm(xs, norm_final, _tile(ns, 256)).reshape(bd, s, d)
    return (y_prompt, y_sample) + tuple(jnp.stack(o) for o in outs)
```

```python
import functools

import jax
import jax.numpy as jnp
import numpy as np
from jax import lax
from jax.experimental import pallas as pl
from jax.experimental.pallas import tpu as pltpu

HEAD_DIM = 64
FOX_HEADS = 8
NSA_HEADS = 8
NSA_KV = 2
NSA_HPG = NSA_HEADS // NSA_KV
FOX_W = FOX_HEADS * HEAD_DIM
NSA_W = NSA_HEADS * HEAD_DIM
NSA_KV_W = NSA_KV * HEAD_DIM
CMP_LEN = 32
CMP_STRIDE = 16
SEL_BLOCK = 64
N_SELECT = 16
WINDOW = 512
FORCED_SCORE = 1e4
ROPE_THETA = 10000.0
ATTN_SCALE = HEAD_DIM ** -0.5
PEER_KEYS = 128
PEER_HEADS = 8
PEER_QDIM = 256
PEER_TOPK = 16
RMS_EPS = 1e-6

LANES = 128
VMEM_LIMIT = 56 * 1024 * 1024
BF16 = jnp.bfloat16
F32 = jnp.float32
NEG_INF = float("-inf")


def _cparams(*sem):
    return pltpu.CompilerParams(dimension_semantics=sem, vmem_limit_bytes=VMEM_LIMIT)


def _gelu(x):
    return 0.5 * x * (1.0 + lax.erf(x * np.float32(np.sqrt(0.5))))


def _extract_topk(x, k):
    n_rows = x.shape[0]
    iota = lax.broadcasted_iota(jnp.int32, x.shape, 0).astype(F32)
    rank = jnp.full(x.shape, k, jnp.int32)
    vals, idxs = [], []
    for r in range(k):
        m = jnp.max(x, axis=0, keepdims=True)
        i = jnp.min(jnp.where(x == m, iota, float(n_rows)), axis=0, keepdims=True)
        hit = iota == i
        vals.append(m)
        idxs.append(i)
        rank = jnp.where(hit, r, rank)
        x = jnp.where(hit, NEG_INF, x)
    return vals, idxs, rank


_PEER_CAND = [(a, b) for a in range(PEER_TOPK) for b in range(PEER_TOPK) if (a + 1) * (b + 1) <= PEER_TOPK]


def _peer_route_kernel(xt_ref, g_ref, wqt_ref, sk_ref, ht_ref, r1_ref, cnt_ref, a_ref, b_ref):
    x = xt_ref[...]
    ms = jnp.mean(x * x, axis=0, keepdims=True)
    h = (x * lax.rsqrt(ms + RMS_EPS)) * g_ref[...]
    hb = h.astype(BF16)
    ht_ref[...] = hb
    q = jnp.dot(wqt_ref[...], hb, preferred_element_type=F32)
    half = PEER_QDIM // 2
    tn = x.shape[1]
    lane_w = min(LANES, tn)
    for hd, cb in [(hd, cb) for hd in range(PEER_HEADS) for cb in range(tn // lane_w)]:
        cols = slice(cb * lane_w, (cb + 1) * lane_w)
        q0 = q[hd * PEER_QDIM: hd * PEER_QDIM + half, cols].astype(BF16)
        q1 = q[hd * PEER_QDIM + half: (hd + 1) * PEER_QDIM, cols].astype(BF16)
        s0 = jnp.dot(sk_ref[0], q0, preferred_element_type=F32)
        s1 = jnp.dot(sk_ref[1], q1, preferred_element_type=F32)
        v0, i0, _ = _extract_topk(s0, PEER_TOPK)
        v1, _, rank1 = _extract_topk(s1, PEER_TOPK)
        cand = jnp.concatenate([v0[a] + v1[b] for a, b in _PEER_CAND], axis=0)
        fv, _, crank = _extract_topk(cand, PEER_TOPK)
        sel = crank < PEER_TOPK
        z = jnp.zeros_like(fv[0])
        for r in range(PEER_TOPK):
            z = z + jnp.exp(fv[r] - fv[0])
        inv_z = 1.0 / z
        row = 0
        iota = lax.broadcasted_iota(jnp.int32, s0.shape, 0).astype(F32)
        cnt_full = jnp.zeros(s0.shape, F32)
        for a in range(PEER_TOPK):
            nb = PEER_TOPK // (a + 1)
            cnt_a = jnp.sum(sel[row:row + nb].astype(F32), axis=0, keepdims=True)
            row += nb
            cnt_full = jnp.where(iota == i0[a], cnt_a, cnt_full)
        sl = slice(hd * PEER_KEYS, (hd + 1) * PEER_KEYS)
        r1_ref[sl, cols] = rank1.astype(F32).astype(BF16)
        cnt_ref[sl, cols] = cnt_full
        a_ref[sl, cols] = jnp.exp(s0 - v0[0])
        b_ref[sl, cols] = (jnp.exp(s1 - v1[0]) * inv_z).astype(BF16)


def _peer_dense_kernel(xt_ref, ht_ref, r1_ref, cnt_ref, a_ref, b_ref, u_ref, vt_ref, o_ref, *, rows_per_step):
    c = pl.program_id(1)
    pair = 2 * PEER_KEYS
    n_pairs = rows_per_step // 2

    @pl.when(c == 0)
    def _():
        o_ref[...] = xt_ref[...]

    row_dot = lambda j: jnp.dot(u_ref[j * PEER_KEYS:(j + 1) * PEER_KEYS, :], ht_ref[...],
                                preferred_element_type=F32)
    acc = None
    pre_next = row_dot(0)
    for jp in range(n_pairs):
        parts = []
        for j in range(2 * jp, 2 * jp + 2):
            pre = pre_next
            if j + 1 < rows_per_step:
                pre_next = row_dot(j + 1)
            act = _gelu(pre)
            i1 = c * rows_per_step + j
            g = jnp.zeros(act.shape, BF16)
            for hd in range(PEER_HEADS):
                sl = slice(hd * PEER_KEYS, (hd + 1) * PEER_KEYS)
                cnt_row = cnt_ref[pl.ds(hd * PEER_KEYS + i1, 1), :].astype(BF16)
                a_row = a_ref[pl.ds(hd * PEER_KEYS + i1, 1), :].astype(BF16)
                g = g + jnp.where(r1_ref[sl, :] < cnt_row, b_ref[sl, :] * a_row, jnp.zeros_like(g))
            parts.append(g * act.astype(BF16))
        rows = slice(jp * pair, (jp + 1) * pair)
        part = jnp.dot(vt_ref[:, rows], jnp.concatenate(parts, axis=0), preferred_element_type=F32)
        acc = part if acc is None else acc + part
    o_ref[...] += acc


def _peer_channel_t(xt, g_ffn, wqt, sk, u_bf, vt_bf, tn, rows_per_step=16):
    d, n = xt.shape
    nt = n // tn
    hk = PEER_HEADS * PEER_KEYS
    tok = lambda i: (0, i)
    full = lambda i: (0, 0)
    ht, r1, cnt, a, b = pl.pallas_call(
        _peer_route_kernel,
        grid=(nt,),
        in_specs=[pl.BlockSpec((d, tn), tok), pl.BlockSpec((d, 1), full),
                  pl.BlockSpec(wqt.shape, full), pl.BlockSpec(sk.shape, lambda i: (0, 0, 0))],
        out_specs=[pl.BlockSpec((d, tn), tok)] + [pl.BlockSpec((hk, tn), tok)] * 4,
        out_shape=[jax.ShapeDtypeStruct((d, n), BF16)] + [jax.ShapeDtypeStruct((hk, n), dt) for dt in (BF16, F32, F32, BF16)],
        compiler_params=_cparams("parallel"),
    )(xt, g_ffn.reshape(d, 1), wqt, sk)
    ec = rows_per_step * PEER_KEYS
    n_chunks = u_bf.shape[0] // ec
    tok2 = lambda i, c: (0, i)
    return pl.pallas_call(
        functools.partial(_peer_dense_kernel, rows_per_step=rows_per_step),
        grid=(nt, n_chunks),
        in_specs=[pl.BlockSpec((d, tn), tok2), pl.BlockSpec((d, tn), tok2)]
                 + [pl.BlockSpec((hk, tn), tok2)] * 4
                 + [pl.BlockSpec((ec, d), lambda i, c: (c, 0)), pl.BlockSpec((d, ec), lambda i, c: (0, c))],
        out_specs=pl.BlockSpec((d, tn), tok2),
        out_shape=jax.ShapeDtypeStruct((d, n), F32),
        compiler_params=_cparams("parallel", "arbitrary"),
    )(xt, ht, r1, cnt, a, b, u_bf, vt_bf)


_C_FQ = 0
_C_FKV = _C_FQ + FOX_W
_C_MISC = _C_FKV + 2 * FOX_W
_C_NQ = _C_MISC + LANES
_C_NQR = _C_NQ + NSA_W
_C_NKV = _C_NQR + NSA_W
_C_NKR = _C_NKV + 6 * NSA_KV_W
_C_END = _C_NKR + 3 * NSA_KV_W
N_GATES = 3 * NSA_HEADS


def _rot_cols(w):
    d, c = w.shape
    half = HEAD_DIM // 2
    w4 = w.reshape(d, c // HEAD_DIM, 2, half)
    return jnp.stack([-w4[:, :, 1], w4[:, :, 0]], axis=2).reshape(d, c)


_NSA_PAIR_ORDER = [g * NSA_HPG + j for j in range(NSA_HPG) for g in range(NSA_KV)]


def _pair_perm(n_per_head):
    return np.concatenate([np.arange(h * n_per_head, (h + 1) * n_per_head) for h in _NSA_PAIR_ORDER])


def _prep_w_in(w_in, fox_fb):
    d = w_in.shape[0]
    o = 0
    fq = w_in[:, o:o + FOX_W]; o += FOX_W
    fkv = w_in[:, o:o + 2 * FOX_W]; o += 2 * FOX_W
    ff = w_in[:, o:o + FOX_HEADS]; o += FOX_HEADS
    nq = w_in[:, o:o + NSA_W]; o += NSA_W
    nkv = w_in[:, o:o + 6 * NSA_KV_W]; o += 6 * NSA_KV_W
    ng = w_in[:, o:o + N_GATES]
    misc = jnp.concatenate([ff, ng, jnp.zeros((d, LANES - FOX_HEADS - N_GATES), w_in.dtype)], axis=1)
    nq = nq[:, _pair_perm(HEAD_DIM)]
    nk = nkv.reshape(d, 3, 2, NSA_KV_W)[:, :, 0].reshape(d, 3 * NSA_KV_W)
    w_all = jnp.concatenate([fq, fkv, misc, nq, _rot_cols(nq), nkv, _rot_cols(nk)], axis=1).astype(BF16)
    fb = jnp.concatenate([fox_fb.astype(F32), jnp.zeros((LANES - FOX_HEADS,), F32)]).reshape(1, LANES)
    return w_all, fb


def _rope_tables(pos):
    half = HEAD_DIM // 2
    inv = ROPE_THETA ** (-jnp.arange(half, dtype=F32) / half)
    ang = pos.astype(F32)[:, None] * inv[None, :]
    reps = LANES // half
    return jnp.tile(jnp.cos(ang), (1, reps)), jnp.tile(jnp.sin(ang), (1, reps))


def _proj_kernel(x_ref, g_ref, w_ref, fb_ref, cos_ref, sin_ref,
                 fq_ref, fkv_ref, fkvb_ref, misc_ref, nq_ref, nkv_ref, nkvb_ref):
    x = x_ref[...]
    ms = jnp.mean(x * x, axis=-1, keepdims=True)
    hb = ((x * lax.rsqrt(ms + RMS_EPS)) * g_ref[...]).astype(BF16)

    def mm(c0, c1):
        return jnp.dot(hb, w_ref[:, c0:c1], preferred_element_type=F32)

    fq_ref[...] = (mm(_C_FQ, _C_FKV) * ATTN_SCALE).astype(BF16)
    fkv = mm(_C_FKV, _C_MISC)
    fkv_ref[...] = fkv
    fkvb_ref[...] = fkv.astype(BF16)
    z = mm(_C_MISC, _C_NQ) + fb_ref[...]
    lane = lax.broadcasted_iota(jnp.int32, z.shape, 1)
    log_sig = jnp.minimum(z, 0.0) - jnp.log1p(jnp.exp(-jnp.abs(z)))
    misc_ref[...] = jnp.where(lane < FOX_HEADS, log_sig, jax.nn.sigmoid(z))
    cos = cos_ref[...]
    sin = sin_ref[...]
    reps = NSA_W // LANES
    cos_q = jnp.concatenate([cos] * reps, axis=1)
    sin_q = jnp.concatenate([sin] * reps, axis=1)
    nq = mm(_C_NQ, _C_NQR) * cos_q + mm(_C_NQR, _C_NKV) * sin_q
    nq_ref[...] = (nq * ATTN_SCALE).astype(BF16)
    for s in range(6):
        c0 = _C_NKV + s * NSA_KV_W
        v = mm(c0, c0 + NSA_KV_W)
        if s % 2 == 0:
            r0 = _C_NKR + (s // 2) * NSA_KV_W
            v = v * cos + mm(r0, r0 + NSA_KV_W) * sin
        nkv_ref[:, s * NSA_KV_W:(s + 1) * NSA_KV_W] = v
        nkvb_ref[:, s * NSA_KV_W:(s + 1) * NSA_KV_W] = v.astype(BF16)


def _project(x2, g_mix, w_all, fb, cos, sin, tm):
    n, d = x2.shape
    tok = lambda i: (i, 0)
    full = lambda i: (0, 0)
    widths = [(FOX_W, BF16), (2 * FOX_W, F32), (2 * FOX_W, BF16), (LANES, F32), (NSA_W, BF16),
              (6 * NSA_KV_W, F32), (6 * NSA_KV_W, BF16)]
    return pl.pallas_call(
        _proj_kernel,
        grid=(n // tm,),
        in_specs=[pl.BlockSpec((tm, d), tok), pl.BlockSpec((1, d), full), pl.BlockSpec(w_all.shape, full),
                  pl.BlockSpec((1, LANES), full), pl.BlockSpec((tm, LANES), tok), pl.BlockSpec((tm, LANES), tok)],
        out_specs=[pl.BlockSpec((tm, w), tok) for w, _ in widths],
        out_shape=[jax.ShapeDtypeStruct((n, w), dt) for w, dt in widths],
        compiler_params=_cparams("parallel"),
    )(x2, g_mix.reshape(1, d), w_all, fb, cos, sin)


def _merge_kernel(x_ref, of_ref, on_ref, gf_ref, gn_ref, wf_ref, wn_ref, o_ref):
    def norm(v, g):
        v = v.astype(F32)
        ms = jnp.mean(v * v, axis=-1, keepdims=True)
        return ((v * lax.rsqrt(ms + RMS_EPS)) * g).astype(BF16)

    yf = norm(of_ref[...], gf_ref[...])
    yn = norm(on_ref[...], gn_ref[...])
    o_ref[...] = (x_ref[...] + jnp.dot(yf, wf_ref[...], preferred_element_type=F32)
                  + jnp.dot(yn, wn_ref[...], preferred_element_type=F32))


def _merge(x2, of, on, g_fox, g_nsa, w_f, w_n, tm):
    n, d = x2.shape
    tok = lambda i: (i, 0)
    full = lambda i: (0, 0)
    return pl.pallas_call(
        _merge_kernel,
        grid=(n // tm,),
        in_specs=[pl.BlockSpec((tm, d), tok), pl.BlockSpec((tm, FOX_W), tok), pl.BlockSpec((tm, NSA_W), tok),
                  pl.BlockSpec((1, FOX_W), full), pl.BlockSpec((1, NSA_W), full),
                  pl.BlockSpec((FOX_W, d), full), pl.BlockSpec((NSA_W, d), full)],
        out_specs=pl.BlockSpec((tm, d), tok),
        out_shape=jax.ShapeDtypeStruct((n, d), F32),
        compiler_params=_cparams("parallel"),
    )(x2, of, on, g_fox.reshape(1, -1), g_nsa.reshape(1, -1), w_f, w_n)


_NT = (((1,), (1,)), ((), ()))


def _flash_init(m_ref, l_ref, acc_ref):
    m_ref[...] = jnp.full(m_ref.shape, NEG_INF, F32)
    l_ref[...] = jnp.zeros(l_ref.shape, F32)
    acc_ref[...] = jnp.zeros(acc_ref.shape, F32)


def _flash_steps(qhs, load_k, load_v, bias_fn, lo, hi, tk, m_ref, l_ref, acc_ref):
    def body(kt, carry):
        ks = pl.multiple_of(kt * tk, tk)
        k = load_k(ks)
        v = load_v(ks)
        biases = bias_fn(ks)
        for n, qh in enumerate(qhs):
            s = lax.dot_general(qh, k, _NT, preferred_element_type=F32) + biases[n]
            m_prev = m_ref[n]
            m_new = jnp.maximum(m_prev, jnp.max(s, axis=1, keepdims=True))
            m_safe = jnp.where(m_new == NEG_INF, 0.0, m_new)
            p = jnp.exp(s - _lane_repeat(m_safe, tk))
            alpha = jnp.exp(m_prev - m_safe)
            l_ref[n] = alpha * l_ref[n] + jnp.sum(p, axis=1, keepdims=True)
            acc_ref[n] = alpha * acc_ref[n] + jnp.dot(p.astype(BF16), v, preferred_element_type=F32)
            m_ref[n] = m_new
        return carry

    lax.fori_loop(lo, hi, body, 0)


def _lane_repeat(x, width):
    return jnp.concatenate([x] * (width // LANES), axis=1)


def _flash_finish(l_ref, acc_ref):
    return acc_ref[...] / jnp.maximum(l_ref[...], 1e-30)


def _half_mask(shape, hh):
    lane = lax.broadcasted_iota(jnp.int32, shape, 1)
    return (lane >= HEAD_DIM) if hh else (lane < HEAD_DIM)


def _causal_bias(bias, qstart, ks):
    qpos = qstart + lax.broadcasted_iota(jnp.int32, bias.shape, 0)
    kpos = ks + lax.broadcasted_iota(jnp.int32, bias.shape, 1)
    return jnp.where(kpos <= qpos, bias, NEG_INF)


def _fox_prompt_kernel(q_ref, k_ref, v_ref, cq_ref, ck_ref, o_ref, m_ref, l_ref, acc_ref, *, tq):
    j = pl.program_id(1)
    qi = pl.program_id(2)
    qstart = qi * tq
    qp = q_ref[0]
    load_k = lambda ks: k_ref[0, pl.ds(ks, tq), :]
    load_v = lambda ks: v_ref[0, pl.ds(ks, tq), :]
    halves = [_half_mask(qp.shape, hh) for hh in range(2)]
    qhs = [jnp.where(half, qp, jnp.zeros_like(qp)) for half in halves]
    cqs = [_lane_repeat(jnp.broadcast_to(cq_ref[0, 0, :, hh:hh + 1], (tq, LANES)), tq) for hh in range(2)]
    bias = lambda ks: [cqs[hh] - ck_ref[0, pl.ds(2 * j + hh, 1), pl.ds(ks, tq)] for hh in range(2)]
    bias_diag = lambda ks: [_causal_bias(bb, qstart, ks) for bb in bias(ks)]
    _flash_init(m_ref, l_ref, acc_ref)
    _flash_steps(qhs, load_k, load_v, bias, 0, qi, tq, m_ref, l_ref, acc_ref)
    _flash_steps(qhs, load_k, load_v, bias_diag, qi, qi + 1, tq, m_ref, l_ref, acc_ref)
    o = _flash_finish(l_ref, acc_ref)
    o_ref[0] = jnp.where(halves[0], o[0], o[1])


def _fox_prompt(fq, fkvb, c, tq):
    b, t, _ = fq.shape
    n_pairs = FOX_W // LANES
    cq = c.reshape(b, t, n_pairs, 2).transpose(0, 2, 1, 3)
    ck = c.transpose(0, 2, 1)
    return pl.pallas_call(
        functools.partial(_fox_prompt_kernel, tq=tq),
        grid=(b, n_pairs, t // tq),
        in_specs=[pl.BlockSpec((1, tq, LANES), lambda bi, j, qi: (bi, qi, j)),
                  pl.BlockSpec((1, t, LANES), lambda bi, j, qi: (bi, 0, j)),
                  pl.BlockSpec((1, t, LANES), lambda bi, j, qi: (bi, 0, n_pairs + j)),
                  pl.BlockSpec((1, 1, tq, 2), lambda bi, j, qi: (bi, j, qi, 0)),
                  pl.BlockSpec((1, FOX_HEADS, t), lambda bi, j, qi: (bi, 0, 0))],
        out_specs=pl.BlockSpec((1, tq, LANES), lambda bi, j, qi: (bi, qi, j)),
        out_shape=jax.ShapeDtypeStruct((b, t, FOX_W), F32),
        scratch_shapes=[pltpu.VMEM((2, tq, LANES), F32), pltpu.VMEM((2, tq, LANES), F32), pltpu.VMEM((2, tq, LANES), F32)],
        compiler_params=_cparams("parallel", "parallel", "arbitrary"),
    )(fq, fkvb, fkvb, cq, ck)


def _round_up(x, m):
    return -(-x // m) * m


def _sel_cover_t(nc, nb, ncp):
    cs = np.arange(ncp)[None, :] * CMP_STRIDE
    bs = np.arange(nb)[:, None] * SEL_BLOCK
    cov = np.clip(np.minimum(cs + CMP_LEN, bs + SEL_BLOCK) - np.maximum(cs, bs), 0, None)
    cov = np.where(np.arange(ncp)[None, :] < nc, cov, 0)
    return jnp.asarray(cov.astype(np.float32) / CMP_LEN, BF16)


def _nsa_prompt_kernel(q_ref, kv_ref, kc_ref, vc_ref, misc_ref, covt_ref, o_ref,
                       qh_ref, oc_ref, bias_ref, m_ref, l_ref, acc_ref, *, tq, nc, nb, n_sel):
    qi = pl.program_id(1)
    qstart = qi * tq
    ncp = kc_ref.shape[1]
    n_pairs = NSA_W // LANES
    load = lambda slot: (lambda ks: kv_ref[0, pl.ds(ks, tq), slot * LANES:(slot + 1) * LANES])

    cidx = lax.broadcasted_iota(jnp.int32, (tq, ncp), 1)
    qpos_c = qstart + lax.broadcasted_iota(jnp.int32, (tq, ncp), 0)
    cmask = (cidx * CMP_STRIDE + (CMP_LEN - 1) <= qpos_c) & (cidx < nc)
    psum = [jnp.zeros((tq, ncp), F32) for _ in range(NSA_KV)]
    for j in range(n_pairs):
        qp = q_ref[0, :, j * LANES:(j + 1) * LANES]
        for g in range(NSA_KV):
            h = j * NSA_KV + g
            qh = jnp.where(_half_mask(qp.shape, g), qp, jnp.zeros_like(qp))
            qh_ref[h] = qh
            s = lax.dot_general(qh, kc_ref[0], _NT, preferred_element_type=F32)
            s = jnp.where(cmask, s, NEG_INF)
            m = jnp.max(s, axis=1, keepdims=True)
            m = jnp.where(m == NEG_INF, 0.0, m)
            e = jnp.exp(s - m)
            p = e / jnp.maximum(jnp.sum(e, axis=1, keepdims=True), 1e-30)
            psum[g] = psum[g] + p
            oc_ref[h] = jnp.dot(p.astype(BF16), vc_ref[0], preferred_element_type=F32)

    for g in range(NSA_KV):
        hi = psum[g].astype(BF16)
        lo = (psum[g] - hi.astype(F32)).astype(BF16)
        pb = (lax.dot_general(covt_ref[...], hi, _NT, preferred_element_type=F32)
              + lax.dot_general(covt_ref[...], lo, _NT, preferred_element_type=F32))
        jj = lax.broadcasted_iota(jnp.int32, (nb, tq), 0)
        cur = (qstart + lax.broadcasted_iota(jnp.int32, (nb, tq), 1)) // SEL_BLOCK
        causal = jj <= cur
        forced = (jj == 0) | (jj == cur) | (jj == cur - 1)
        score = jnp.where(forced, FORCED_SCORE, jnp.where(causal, pb, NEG_INF))
        _, _, rank = _extract_topk(score, n_sel)
        sel_t = jnp.where((rank < n_sel) & causal, 1.0, 0.0)
        sel_t = jnp.concatenate([sel_t, jnp.zeros((LANES - nb, tq), F32)], axis=0)
        sel = sel_t.T.astype(BF16)

        def fill(kt, carry):
            ks = pl.multiple_of(kt * tq, tq)
            blk = (ks + lax.broadcasted_iota(jnp.int32, (LANES, tq), 1)) // SEL_BLOCK
            expand = jnp.where(lax.broadcasted_iota(jnp.int32, (LANES, tq), 0) == blk, 1.0, 0.0).astype(BF16)
            hit = jnp.dot(sel, expand, preferred_element_type=F32)
            bias = jnp.where(hit > 0.5, 0.0, NEG_INF)
            bias_ref[g, :, pl.ds(ks, tq)] = _causal_bias(bias, qstart, ks)
            return carry

        lax.fori_loop(0, qi + 1, fill, 0)

    def win_bias(ks):
        qpos = qstart + lax.broadcasted_iota(jnp.int32, (tq, tq), 0)
        kpos = ks + lax.broadcasted_iota(jnp.int32, (tq, tq), 1)
        dist = qpos - kpos
        return jnp.where((dist >= 0) & (dist < WINDOW), 0.0, NEG_INF)

    win_lo = jnp.maximum(qi - (-(-WINDOW // tq)), 0)
    sel_bias = lambda ks: [bias_ref[g, :, pl.ds(ks, tq)] for g in range(NSA_KV)]
    win_bias2 = lambda ks: [win_bias(ks)] * NSA_KV
    for j in range(n_pairs):
        qhs = [qh_ref[j * NSA_KV + g] for g in range(NSA_KV)]
        _flash_init(m_ref, l_ref, acc_ref)
        _flash_steps(qhs, load(2), load(3), sel_bias, 0, qi + 1, tq, m_ref, l_ref, acc_ref)
        o_s = _flash_finish(l_ref, acc_ref)
        _flash_init(m_ref, l_ref, acc_ref)
        _flash_steps(qhs, load(4), load(5), win_bias2, win_lo, qi + 1, tq, m_ref, l_ref, acc_ref)
        o_w = _flash_finish(l_ref, acc_ref)
        outs = []
        for g in range(NSA_KV):
            col = FOX_HEADS + (g * NSA_HPG + j) * 3
            gates = [misc_ref[0, :, col + r:col + r + 1] for r in range(3)]
            outs.append(gates[0] * oc_ref[j * NSA_KV + g] + gates[1] * o_s[g] + gates[2] * o_w[g])
        o_ref[0, :, j * LANES:(j + 1) * LANES] = jnp.where(_half_mask(outs[0].shape, 0), outs[0], outs[1])


def _nsa_prompt(nq, nkvb, kc, vc, misc, tq):
    b, t, _ = nq.shape
    nc = kc.shape[1] - 1
    ncp = _round_up(nc + 1, LANES)
    nb = -(-t // SEL_BLOCK)
    assert nb <= LANES and t % tq == 0 and tq % SEL_BLOCK == 0
    kcp = _pad_rows(kc, ncp)
    vcp = _pad_rows(vc, ncp)
    covt = _sel_cover_t(nc, nb, ncp)
    kern = functools.partial(_nsa_prompt_kernel, tq=tq, nc=nc, nb=nb, n_sel=min(N_SELECT, nb))
    seq = lambda bi, qi: (bi, 0, 0)
    tile = lambda bi, qi: (bi, qi, 0)
    return pl.pallas_call(
        kern,
        grid=(b, t // tq),
        in_specs=[pl.BlockSpec((1, tq, NSA_W), tile), pl.BlockSpec((1, t, 6 * NSA_KV_W), seq),
                  pl.BlockSpec((1, ncp, LANES), seq), pl.BlockSpec((1, ncp, LANES), seq),
                  pl.BlockSpec((1, tq, LANES), tile), pl.BlockSpec((nb, ncp), lambda bi, qi: (0, 0))],
        out_specs=pl.BlockSpec((1, tq, NSA_W), tile),
        out_shape=jax.ShapeDtypeStruct((b, t, NSA_W), F32),
        scratch_shapes=[pltpu.VMEM((NSA_HEADS, tq, LANES), BF16), pltpu.VMEM((NSA_HEADS, tq, LANES), F32),
                        pltpu.VMEM((NSA_KV, tq, t), F32),
                        pltpu.VMEM((NSA_KV, tq, LANES), F32), pltpu.VMEM((NSA_KV, tq, LANES), F32),
                        pltpu.VMEM((NSA_KV, tq, LANES), F32)],
        compiler_params=_cparams("parallel", "arbitrary"),
    )(nq, nkvb, kcp, vcp, misc, covt)


def _tile(n, pref):
    t = min(pref, n)
    while n % t:
        t //= 2
    return t


def _mixer_prompt(x, g_mix, w_all, fb, cmp_weights):
    b, t, d = x.shape
    n = b * t
    cos, sin = _rope_tables(jnp.arange(t))
    cos = jnp.tile(cos, (b, 1))
    sin = jnp.tile(sin, (b, 1))
    fq, fkv, fkvb, misc, nq, nkv, nkvb = _project(x.reshape(n, d), g_mix, w_all, fb, cos, sin, _tile(n, 512))
    logf = misc[:, :FOX_HEADS].reshape(b, t, FOX_HEADS)
    c = jnp.cumsum(logf, axis=1)
    tq = _tile(t, 512)
    of = _fox_prompt(fq.reshape(b, t, FOX_W), fkvb.reshape(b, t, 2 * FOX_W), c, tq)
    rows = nkv.reshape(b, t, 6, NSA_KV, HEAD_DIM)
    page_rows = _tile(t, LANES)
    pages = jnp.arange(n // page_rows, dtype=jnp.int32).reshape(b, t // page_rows)
    kc, vc = _nsa_cmp_sample(0, pages, nkv.reshape(1, n // page_rows, page_rows, 6 * NSA_KV_W), *cmp_weights,
                             _tile(t // page_rows, 16), row_major=True)
    on = _nsa_prompt(nq.reshape(b, t, NSA_W), nkvb.reshape(b, t, 6 * NSA_KV_W), kc, vc, misc.reshape(b, t, LANES), tq)
    wb = min(WINDOW, t)
    return (of, on, fkv.reshape(b, t, 2, FOX_HEADS, HEAD_DIM), logf, rows[:, :, :4], rows[:, t - wb:, 4:])


def _split3(x):
    hi = x.astype(BF16)
    r = x - hi.astype(F32)
    mid = r.astype(BF16)
    return hi, mid, (r - mid.astype(F32)).astype(BF16)


def _split2(x):
    hi = x.astype(BF16)
    return hi, (x - hi.astype(F32)).astype(BF16)


def _softmax_update(s_list, v_list, m_ref, l_ref, acc_ref, v_transposed=False):
    pv = (lambda p, v: lax.dot_general(p, v, _NT, preferred_element_type=F32)) if v_transposed else (
        lambda p, v: jnp.dot(p, v, preferred_element_type=F32))
    m_prev = m_ref[...]
    m_new = m_prev
    for s in s_list:
        m_new = jnp.maximum(m_new, jnp.max(s, axis=1, keepdims=True))
    m_safe = jnp.where(m_new == NEG_INF, 0.0, m_new)
    alpha = jnp.exp(m_prev - m_safe)
    l_new = alpha * l_ref[...]
    acc = alpha * acc_ref[...]
    for s, v_bf in zip(s_list, v_list):
        p = jnp.exp(s - m_safe)
        l_new = l_new + jnp.sum(p, axis=1, keepdims=True)
        acc = acc + pv(p.astype(BF16), v_bf)
    l_ref[...] = l_new
    acc_ref[...] = acc
    m_ref[...] = m_new


def _fox_sample_kernel(pt_ref, q_ref, cnew_ref, knew_ref, vnew_ref, bnew_ref, scan_ref, *rest, n_pg, s_tok):
    kv_refs, lf_refs = rest[:n_pg], rest[n_pg:2 * n_pg]
    o_ref, m_ref, l_ref, acc_ref, carry_ref = rest[2 * n_pg:]
    st = pl.program_id(1)

    @pl.when(st == 0)
    def _():
        _flash_init(m_ref, l_ref, acc_ref)
        carry_ref[...] = jnp.zeros(carry_ref.shape, F32)

    q = q_ref[0]
    cnew = cnew_ref[0]
    page_rows = kv_refs[0].shape[2]
    carry = carry_ref[...]
    s_list, v_list = [], []
    for i in reversed(range(n_pg)):
        x = lf_refs[i][...]
        sc = sum(jnp.dot(p, scan_ref[...], preferred_element_type=F32) for p in _split3(x))
        d = sc[:, :page_rows] + carry
        carry = carry + sc[:, page_rows:]
        dd = jnp.concatenate([jnp.broadcast_to(d[h:h + 1], (s_tok, page_rows)) for h in range(FOX_HEADS)], axis=0)
        s_list.append(jnp.dot(q, kv_refs[i][0].astype(BF16), preferred_element_type=F32) + (cnew + dd))
        v_list.append(kv_refs[i][1].astype(BF16))
    carry_ref[...] = carry
    _softmax_update(s_list, v_list, m_ref, l_ref, acc_ref, v_transposed=True)

    @pl.when(st == pl.num_programs(1) - 1)
    def _():
        s = lax.dot_general(q, knew_ref[0], _NT, preferred_element_type=F32) + bnew_ref[0]
        _softmax_update([s], [vnew_ref[0]], m_ref, l_ref, acc_ref)
        o = _flash_finish(l_ref, acc_ref)
        lane_head = lax.broadcasted_iota(jnp.int32, (s_tok, FOX_W), 1) // HEAD_DIM
        out = jnp.zeros((s_tok, FOX_W), F32)
        for h in range(FOX_HEADS):
            out = jnp.where(lane_head == h, o[h * s_tok:(h + 1) * s_tok], out)
        o_ref[0] = out


def _fox_sample(layer, page_table, q_rows, cnew_b, knew, vnew, bnew, cache_kv, logf_t, n_pg):
    bd, n_pages = page_table.shape
    page_rows = cache_kv.shape[4]
    s_tok = q_rows.shape[1] // FOX_HEADS
    rows = q_rows.shape[1]
    kidx = np.arange(page_rows)
    scan = np.concatenate([(kidx[:, None] > kidx[None, :]), np.ones((page_rows, page_rows), bool)], axis=1)
    scan = jnp.asarray(scan, BF16)
    n_st = n_pages // n_pg
    seq = lambda b, st, pt: (b, 0, 0)

    def page_map(i, nd):
        return lambda b, st, pt: (layer, pt[b, n_pages - (st + 1) * n_pg + i]) + (0,) * nd

    grid_spec = pltpu.PrefetchScalarGridSpec(
        num_scalar_prefetch=1,
        grid=(bd, n_st),
        in_specs=[pl.BlockSpec((1, rows, FOX_W), seq), pl.BlockSpec((1, rows, page_rows), seq),
                  pl.BlockSpec((1, page_rows, FOX_W), seq), pl.BlockSpec((1, page_rows, FOX_W), seq),
                  pl.BlockSpec((1, rows, page_rows), seq),
                  pl.BlockSpec(scan.shape, lambda b, st, pt: (0, 0))]
                 + [pl.BlockSpec((None, None, 2, FOX_W, page_rows), page_map(i, 3)) for i in range(n_pg)]
                 + [pl.BlockSpec((None, None, FOX_HEADS, page_rows), page_map(i, 2)) for i in range(n_pg)],
        out_specs=pl.BlockSpec((1, s_tok, FOX_W), seq),
        scratch_shapes=[pltpu.VMEM((rows, 1), F32), pltpu.VMEM((rows, 1), F32), pltpu.VMEM((rows, FOX_W), F32),
                        pltpu.VMEM((FOX_HEADS, page_rows), F32)])
    return pl.pallas_call(
        functools.partial(_fox_sample_kernel, n_pg=n_pg, s_tok=s_tok),
        grid_spec=grid_spec,
        out_shape=jax.ShapeDtypeStruct((bd, s_tok, FOX_W), F32),
        compiler_params=_cparams("parallel", "arbitrary"),
    )(page_table, q_rows, cnew_b, knew, vnew, bnew, scan, *([cache_kv] * n_pg), *([logf_t] * n_pg))


def _nsa_cmp_sample_kernel(pt_ref, pe_ref, w1_ref, w2_ref, *rest, n_pg, row_major):
    pages = rest[:n_pg]
    kc_ref, vc_ref, carry_ref, xs_ref = rest[n_pg:]
    st = pl.program_id(1)

    @pl.when(st == 0)
    def _():
        carry_ref[...] = jnp.zeros(carry_ref.shape, F32)

    page_rows = pages[0].shape[0] if row_major else pages[0].shape[2]
    m = n_pg * page_rows // CMP_STRIDE
    last = lax.broadcasted_iota(jnp.int32, (m, NSA_KV_W), 0) == m - 1
    for slot, out_ref in ((0, kc_ref), (1, vc_ref)):
        for i, pg in enumerate(pages):
            xs_ref[i * page_rows:(i + 1) * page_rows, :] = (
                pg[:, slot * NSA_KV_W:(slot + 1) * NSA_KV_W] if row_major else pg[slot].T)
        a = jnp.zeros((m, NSA_KV_W), F32)
        b = jnp.zeros((m, NSA_KV_W), F32)
        for r in range(CMP_STRIDE):
            x = xs_ref[pl.ds(r, m, stride=CMP_STRIDE), :]
            xa = (x + pe_ref[slot, r:r + 1, :]).astype(BF16)
            xb = (x + pe_ref[slot, CMP_STRIDE + r:CMP_STRIDE + r + 1, :]).astype(BF16)
            a = a + jnp.dot(xa, w1_ref[slot, r], preferred_element_type=F32)
            b = b + jnp.dot(xb, w1_ref[slot, CMP_STRIDE + r], preferred_element_type=F32)
        b_next = jnp.where(last, carry_ref[slot, 0:1, :], pltpu.roll(b, m - 1, axis=0))
        carry_ref[slot] = b
        h = _gelu(a + b_next).astype(BF16)
        out_ref[0] = jnp.dot(h, w2_ref[slot], preferred_element_type=F32).astype(BF16)


def _nsa_cmp_sample(layer, page_table, cache_nsa, pe2, w1_bd, w2_bd, n_pg, row_major=False):
    bd, n_pages = page_table.shape
    page_rows = cache_nsa.shape[2] if row_major else cache_nsa.shape[4]
    cpp = page_rows // CMP_STRIDE
    m = n_pg * cpp
    n_st = n_pages // n_pg
    const = lambda nd: (lambda b, st, pt: (0,) * nd)

    def page_map(i):
        return lambda b, st, pt: (layer, pt[b, n_pages - (st + 1) * n_pg + i]) + (0,) * (2 if row_major else 3)

    page_block = (None, None, page_rows, 2 * NSA_KV_W) if row_major else (None, None, 2) + cache_nsa.shape[3:]
    out_spec = pl.BlockSpec((1, m, NSA_KV_W), lambda b, st, pt: (b, n_st - 1 - st, 0))
    grid_spec = pltpu.PrefetchScalarGridSpec(
        num_scalar_prefetch=1,
        grid=(bd, n_st),
        in_specs=[pl.BlockSpec(pe2.shape, const(3)), pl.BlockSpec(w1_bd.shape, const(4)),
                  pl.BlockSpec(w2_bd.shape, const(3))]
                 + [pl.BlockSpec(page_block, page_map(i)) for i in range(n_pg)],
        out_specs=[out_spec, out_spec],
        scratch_shapes=[pltpu.VMEM((2, m, NSA_KV_W), F32), pltpu.VMEM((n_pg * page_rows, NSA_KV_W), F32)])
    shape = jax.ShapeDtypeStruct((bd, n_pages * cpp, NSA_KV_W), BF16)
    return pl.pallas_call(
        functools.partial(_nsa_cmp_sample_kernel, n_pg=n_pg, row_major=row_major),
        grid_spec=grid_spec,
        out_shape=[shape, shape],
        compiler_params=_cparams("parallel", "arbitrary"),
    )(page_table, pe2, w1_bd, w2_bd, *([cache_nsa] * n_pg))


def _nsa_sample_kernel(pt_ref, q_ref, kc_ref, vc_ref, gate_ref, covt_ref, gsum_ref, win_ref, new_ref, *rest,
                       n_pg, s_tok, nc, nb, n_sel, past):
    pages = rest[:n_pg]
    o_ref, bias_ref, oc_ref, m_ref, l_ref, acc_ref = rest[n_pg:]
    st = pl.program_id(1)
    q = q_ref[0]
    rows = q.shape[0]
    page_rows = pages[0].shape[2]

    @pl.when(st == 0)
    def _():
        ncp = kc_ref.shape[1]
        nbp = covt_ref.shape[0]
        s = lax.dot_general(q, kc_ref[0], _NT, preferred_element_type=F32)
        cidx = lax.broadcasted_iota(jnp.int32, (rows, ncp), 1)
        qpos = past + lax.broadcasted_iota(jnp.int32, (rows, ncp), 0) % s_tok
        s = jnp.where((cidx * CMP_STRIDE + (CMP_LEN - 1) <= qpos) & (cidx < nc), s, NEG_INF)
        m = jnp.max(s, axis=1, keepdims=True)
        m = jnp.where(m == NEG_INF, 0.0, m)
        e = jnp.exp(s - m)
        p = e / jnp.maximum(jnp.sum(e, axis=1, keepdims=True), 1e-30)
        oc_ref[...] = jnp.dot(p.astype(BF16), vc_ref[0], preferred_element_type=F32)
        psum = sum(jnp.dot(gsum_ref[...], x, preferred_element_type=F32) for x in _split2(p))
        pb = sum(lax.dot_general(covt_ref[...], x, _NT, preferred_element_type=F32) for x in _split2(psum))
        jj = lax.broadcasted_iota(jnp.int32, (nbp, rows), 0)
        cur = (past + lax.broadcasted_iota(jnp.int32, (nbp, rows), 1) % s_tok) // SEL_BLOCK
        causal = (jj <= cur) & (jj < nb)
        forced = (jj == 0) | (jj == cur) | (jj == cur - 1)
        score = jnp.where(forced, FORCED_SCORE, jnp.where(causal, pb, NEG_INF))
        _, _, rank = _extract_topk(score, n_sel)
        sel = jnp.where((rank < n_sel) & causal, 1.0, 0.0).T.astype(BF16)
        chunk = min(past, 16 * page_rows)

        def fill(c, carry):
            ks = pl.multiple_of(c * chunk, chunk)
            blk = (ks + lax.broadcasted_iota(jnp.int32, (nbp, chunk), 1)) // SEL_BLOCK
            expand = jnp.where(lax.broadcasted_iota(jnp.int32, (nbp, chunk), 0) == blk, 1.0, 0.0).astype(BF16)
            hit = jnp.dot(sel, expand, preferred_element_type=F32)
            bias_ref[:, pl.ds(ks, chunk)] = jnp.where(hit > 0.5, 0.0, NEG_INF)
            return carry

        lax.fori_loop(0, bias_ref.shape[1] // chunk, fill, 0)
        _flash_init(m_ref, l_ref, acc_ref)

    s_list, v_list = [], []
    for i in range(n_pg):
        ks = pl.multiple_of((st * n_pg + i) * page_rows, page_rows)
        s = jnp.dot(q, pages[i][0].astype(BF16), preferred_element_type=F32)
        s_list.append(s + bias_ref[:, pl.ds(ks, page_rows)])
        v_list.append(pages[i][1].astype(BF16))
    _softmax_update(s_list, v_list, m_ref, l_ref, acc_ref, v_transposed=True)

    @pl.when(st == pl.num_programs(1) - 1)
    def _():
        tok = lax.broadcasted_iota(jnp.int32, (rows, page_rows), 0) % s_tok
        col = lax.broadcasted_iota(jnp.int32, (rows, page_rows), 1)
        bnew = jnp.where((col <= tok) & (col < s_tok), 0.0, NEG_INF)
        s = lax.dot_general(q, new_ref[0, 0], _NT, preferred_element_type=F32) + bnew
        _softmax_update([s], [new_ref[0, 1]], m_ref, l_ref, acc_ref)
        o_s = _flash_finish(l_ref, acc_ref)
        wb = win_ref.shape[2]
        wtok = lax.broadcasted_iota(jnp.int32, (rows, wb), 0) % s_tok
        wi = lax.broadcasted_iota(jnp.int32, (rows, wb), 1)
        dist = wb + wtok - wi
        ok = (past - wb + wi >= 0) & (dist >= 0) & (dist < WINDOW)
        s1 = jnp.dot(q, win_ref[0].astype(BF16), preferred_element_type=F32)
        s1 = jnp.where(ok, s1, NEG_INF)
        s2 = lax.dot_general(q, new_ref[0, 2], _NT, preferred_element_type=F32) + bnew
        m = jnp.maximum(jnp.max(s1, axis=1, keepdims=True), jnp.max(s2, axis=1, keepdims=True))
        e1 = jnp.exp(s1 - m)
        e2 = jnp.exp(s2 - m)
        den = jnp.sum(e1, axis=1, keepdims=True) + jnp.sum(e2, axis=1, keepdims=True)
        o_w = (lax.dot_general(e1.astype(BF16), win_ref[1].astype(BF16), _NT, preferred_element_type=F32)
               + jnp.dot(e2.astype(BF16), new_ref[0, 3], preferred_element_type=F32)) / den
        o = gate_ref[0, 0] * oc_ref[...] + gate_ref[0, 1] * o_s + gate_ref[0, 2] * o_w
        for j in range(NSA_W // LANES):
            lo = o[(2 * j) * s_tok:(2 * j + 1) * s_tok]
            hi = o[(2 * j + 1) * s_tok:(2 * j + 2) * s_tok]
            o_ref[0, :, j * LANES:(j + 1) * LANES] = jnp.where(_half_mask(lo.shape, 0), lo, hi)


def _nsa_sample(layer, page_table, q_rows, kc, vc, gates, win_state, new_kv, cache_nsa, s_tok, n_pg):
    bd, n_pages = page_table.shape
    page_rows = cache_nsa.shape[4]
    past = n_pages * page_rows
    rows = q_rows.shape[1]
    ncp = kc.shape[1]
    nc = ncp - 1
    nb = -(-(past + s_tok) // SEL_BLOCK)
    nbp = _round_up(nb, LANES)
    assert past % SEL_BLOCK == 0 and s_tok <= SEL_BLOCK and s_tok < CMP_STRIDE
    covt = _sel_cover_t(nc, nbp, ncp)
    r = np.arange(rows)
    same = (r[:, None] % s_tok == r[None, :] % s_tok) & ((r[:, None] // s_tok) % NSA_KV == (r[None, :] // s_tok) % NSA_KV)
    live = r < NSA_HEADS * s_tok
    gsum = jnp.asarray(same & live[:, None] & live[None, :], BF16)
    wb = win_state.shape[4]
    n_st = n_pages // n_pg
    seq = lambda b, st, pt: (b, 0, 0)
    const2 = lambda b, st, pt: (0, 0)
    grid_spec = pltpu.PrefetchScalarGridSpec(
        num_scalar_prefetch=1,
        grid=(bd, n_st),
        in_specs=[pl.BlockSpec((1, rows, LANES), seq), pl.BlockSpec((1, ncp, LANES), seq),
                  pl.BlockSpec((1, ncp, LANES), seq), pl.BlockSpec((1, 3, rows, LANES), lambda b, st, pt: (b, 0, 0, 0)),
                  pl.BlockSpec(covt.shape, const2), pl.BlockSpec(gsum.shape, const2),
                  pl.BlockSpec((None, None, 2, NSA_KV_W, wb), lambda b, st, pt: (layer, b, 0, 0, 0)),
                  pl.BlockSpec((1, 4, page_rows, LANES), lambda b, st, pt: (b, 0, 0, 0))]
                 + [pl.BlockSpec((None, None, 2) + cache_nsa.shape[3:],
                                 (lambda i: lambda b, st, pt: (layer, pt[b, st * n_pg + i], 1, 0, 0))(i))
                    for i in range(n_pg)],
        out_specs=pl.BlockSpec((1, s_tok, NSA_W), seq),
        scratch_shapes=[pltpu.VMEM((rows, past), F32), pltpu.VMEM((rows, LANES), F32),
                        pltpu.VMEM((rows, 1), F32), pltpu.VMEM((rows, 1), F32), pltpu.VMEM((rows, LANES), F32)])
    kern = functools.partial(_nsa_sample_kernel, n_pg=n_pg, s_tok=s_tok, nc=nc, nb=nb, n_sel=min(N_SELECT, nb),
                             past=past)
    return pl.pallas_call(
        kern,
        grid_spec=grid_spec,
        out_shape=jax.ShapeDtypeStruct((bd, s_tok, NSA_W), F32),
        compiler_params=_cparams("parallel", "arbitrary"),
    )(page_table, q_rows, kc, vc, gates, covt, gsum, win_state, new_kv, *([cache_nsa] * n_pg))


def _block_diag_groups(w):
    eye = jnp.eye(NSA_KV, dtype=w.dtype)
    out = jnp.einsum('gh,...de->...gdhe', eye, w)
    return out.reshape(w.shape[:-2] + (NSA_KV_W, NSA_KV_W))


def _pad_rows(x, rows, value=0.0):
    pad = [(0, 0)] * x.ndim
    pad[-2] = (0, rows - x.shape[-2])
    return jnp.pad(x, pad, constant_values=value)


def _cmp_weights(cmp_pe, cmp_w1, cmp_w2):
    pe2 = jnp.tile(cmp_pe, (1, 1, NSA_KV))
    w1_bd = _block_diag_groups(cmp_w1.reshape(2, CMP_LEN, HEAD_DIM, HEAD_DIM)).astype(BF16)
    return pe2, w1_bd, _block_diag_groups(cmp_w2).astype(BF16)


def _mixer_sample(x, layer, page_table, cache_fox_kv, logf_t, cache_nsa_kv, state_win, state_win_rows, g_mix, w_all,
                  fb, cmp_weights):
    bd, s, d = x.shape
    n_pages = page_table.shape[1]
    page_rows = cache_fox_kv.shape[4]
    past = n_pages * page_rows
    n = bd * s
    pos = past + jnp.arange(s)
    cos, sin = _rope_tables(pos)
    cos = jnp.tile(cos, (bd, 1))
    sin = jnp.tile(sin, (bd, 1))
    fq, fkv, fkvb, misc, nq, nkv, nkvb = _project(x.reshape(n, d), g_mix, w_all, fb, cos, sin, _tile(n, 256))
    logf = misc[:, :FOX_HEADS].reshape(bd, s, FOX_HEADS)
    cnew = jnp.cumsum(logf, axis=1).transpose(0, 2, 1)
    rows_f = FOX_HEADS * s
    cnew_b = jnp.broadcast_to(cnew.reshape(bd, rows_f, 1), (bd, rows_f, page_rows))
    causal = jnp.arange(s)[None, :] <= jnp.arange(s)[:, None]
    bnew = jnp.where(causal, cnew[:, :, :, None] - cnew[:, :, None, :], NEG_INF).reshape(bd, rows_f, s)
    bnew = jnp.pad(bnew, ((0, 0), (0, 0), (0, page_rows - s)), constant_values=NEG_INF)
    head_of_lane = jnp.arange(FOX_W) // HEAD_DIM
    fq3 = fq.reshape(bd, 1, s, FOX_W)
    q_rows = jnp.where(head_of_lane[None, None, None, :] == jnp.arange(FOX_HEADS)[None, :, None, None], fq3,
                       jnp.zeros_like(fq3)).reshape(bd, rows_f, FOX_W)
    fkvb3 = fkvb.reshape(bd, s, 2 * FOX_W)
    knew = _pad_rows(fkvb3[:, :, :FOX_W], page_rows)
    vnew = _pad_rows(fkvb3[:, :, FOX_W:], page_rows)
    of = _fox_sample(layer, page_table, q_rows, cnew_b, knew, vnew, bnew, cache_fox_kv, logf_t, _tile(n_pages, 16))
    n_pg = _tile(n_pages, 32)
    kc, vc = _nsa_cmp_sample(layer, page_table, cache_nsa_kv, *cmp_weights, n_pg)
    rows_n = _round_up(NSA_HEADS * s, LANES)
    nq4 = nq.reshape(bd, s, NSA_W // LANES, 1, LANES).transpose(0, 2, 3, 1, 4)
    half_of_lane = (jnp.arange(LANES) // HEAD_DIM)[None, None, None, None, :]
    qn = jnp.where(half_of_lane == jnp.arange(NSA_KV)[None, None, :, None, None], nq4, jnp.zeros_like(nq4))
    qn = _pad_rows(qn.reshape(bd, NSA_HEADS * s, LANES), rows_n)
    gate = misc[:, FOX_HEADS:FOX_HEADS + N_GATES].reshape(bd, s, NSA_KV, NSA_HPG, 3)
    gate = gate.transpose(0, 4, 3, 2, 1).reshape(bd, 3, NSA_HEADS * s, 1)
    gates = jnp.broadcast_to(_pad_rows(gate, rows_n), (bd, 3, rows_n, LANES))
    nkvb3 = nkvb.reshape(bd, s, 6, NSA_KV_W)
    new_kv = _pad_rows(nkvb3[:, :, 2:6].transpose(0, 2, 1, 3), page_rows)
    on = _nsa_sample(layer, page_table, qn, kc, vc, gates, state_win, new_kv, cache_nsa_kv, s, n_pg)
    rows = nkv.reshape(bd, s, 6, NSA_KV, HEAD_DIM)
    win_all = jnp.concatenate([state_win_rows, rows[:, :, 4:]], axis=1)
    return of, on, fkv.reshape(bd, s, 2, FOX_HEADS, HEAD_DIM), logf, rows[:, :, :4], win_all[:, s:]


def _rmsnorm_kernel(x_ref, g_ref, o_ref):
    x = x_ref[...]
    ms = jnp.mean(x * x, axis=-1, keepdims=True)
    o_ref[...] = (x * lax.rsqrt(ms + RMS_EPS)) * g_ref[...]


def _rmsnorm(x2, g, tm):
    n, d = x2.shape
    return pl.pallas_call(
        _rmsnorm_kernel,
        grid=(n // tm,),
        in_specs=[pl.BlockSpec((tm, d), lambda i: (i, 0)), pl.BlockSpec((1, d), lambda i: (0, 0))],
        out_specs=pl.BlockSpec((tm, d), lambda i: (i, 0)),
        out_shape=jax.ShapeDtypeStruct((n, d), F32),
        compiler_params=_cparams("parallel"),
    )(x2, g.reshape(1, d))


def kernel(x_prompt, x_sample, cache_fox_kv, cache_fox_logf, cache_nsa_kv, state_win_kv, page_table,
           norm_mix, w_in, fox_fb, cmp_pe, cmp_w1, cmp_w2, norm_fox, norm_nsa, w_out,
           norm_ffn, peer_wq, peer_subkeys, peer_u, peer_v, norm_final):
    depth = w_in.shape[0]
    b, t, d = x_prompt.shape
    bd, s, _ = x_sample.shape
    n_seq, n_pages = page_table.shape
    past = n_pages * cache_fox_kv.shape[2]
    np_, ns = b * t, bd * s
    xp = x_prompt.reshape(np_, d)
    xs = x_sample.reshape(ns, d)
    perm = _pair_perm(HEAD_DIM)
    outs = [[] for _ in range(8)]
    pool, page_rows = cache_fox_kv.shape[1:3]
    row_minor = (0, 1, 3, 4, 5, 2)
    fox_kv2 = cache_fox_kv.transpose(row_minor).reshape(depth, pool, 2, FOX_W, page_rows)
    logf_t = cache_fox_logf.transpose(0, 1, 3, 2)
    nsa_kv2 = cache_nsa_kv.transpose(row_minor).reshape(depth, pool, 4, NSA_KV_W, page_rows)
    win2 = state_win_kv.transpose(row_minor).reshape(depth, bd, 2, NSA_KV_W, state_win_kv.shape[2])
    for l in range(depth):
        w_all, fb = _prep_w_in(w_in[l], fox_fb[l])
        cmp_weights = _cmp_weights(cmp_pe[l], cmp_w1[l], cmp_w2[l])
        mp = _mixer_prompt(xp.reshape(b, t, d), norm_mix[l], w_all, fb, cmp_weights)
        msm = _mixer_sample(xs.reshape(bd, s, d), l, page_table, fox_kv2, logf_t, nsa_kv2, win2, state_win_kv[l],
                            norm_mix[l], w_all, fb, cmp_weights)
        w_f = w_out[l, :FOX_W].astype(BF16)
        w_n = w_out[l, FOX_W:][perm].astype(BF16)
        g_n = norm_nsa[l][perm]
        xp = _merge(xp, mp[0].reshape(np_, FOX_W), mp[1].reshape(np_, NSA_W), norm_fox[l], g_n, w_f, w_n,
                    _tile(np_, 512))
        xs = _merge(xs, msm[0].reshape(ns, FOX_W), msm[1].reshape(ns, NSA_W), norm_fox[l], g_n, w_f, w_n,
                    _tile(ns, 256))
        wqt = peer_wq[l].T.astype(BF16)
        sk = peer_subkeys[l].astype(BF16)
        u_bf = peer_u[l].astype(BF16)
        vt_bf = peer_v[l].T.astype(BF16)
        xp = _peer_channel_t(xp.T, norm_ffn[l], wqt, sk, u_bf, vt_bf, _tile(np_, 512)).T
        xs = _peer_channel_t(xs.T, norm_ffn[l], wqt, sk, u_bf, vt_bf, _tile(ns, 256)).T
        for i in range(4):
            outs[i].append(mp[2 + i])
            outs[4 + i].append(msm[2 + i])
    y_prompt = _rmsnorm(xp, norm_final, _tile(np_, 512)).reshape(b, t, d)
    y_sample = _rmsnorm(xs, norm_final, _tile(ns, 256)).reshape(bd, s, d)
    return (y_prompt, y_sample) + tuple(jnp.stack(o) for o in outs)
```
